```python
import math
import jax
import jax.numpy as jnp
from jax import lax
import numpy as np

D_MODEL = 2048
BATCH = 2
SEQ = 4096
DEPTH = 2

CHUNK = 64
MEM_LEN = 256
N_EVEN = (DEPTH + 1) // 2
N_ODD = DEPTH // 2
DEEPNORM_ALPHA = (2 * DEPTH) ** 0.25
DEEPNORM_BETA = (8 * DEPTH) ** -0.25
LN_EPS = 1e-5
MIX_WIDTH = D_MODEL
RWKV_WIDTH = MIX_WIDTH // 2
RWKV_HEAD = 64
RWKV_HEADS = RWKV_WIDTH // RWKV_HEAD
RWKV_DECAY_RANK = 64
RWKV_ICLR_RANK = 64
RWKV_GATE_RANK = 128
RWKV_COLS = 3 * RWKV_WIDTH + RWKV_DECAY_RANK + RWKV_ICLR_RANK + RWKV_GATE_RANK
RWKV_GN_EPS = 64e-5
GDN_WIDTH = MIX_WIDTH - RWKV_WIDTH
GDN_HEAD = 128
GDN_HEADS = GDN_WIDTH // GDN_HEAD
GDN_CONV = 4
GDN_COLS = 4 * GDN_WIDTH + 2 * GDN_HEADS
EVEN_COLS = RWKV_COLS + GDN_COLS
CONV_WIDTH = MIX_WIDTH // 2
CONV_KERNEL = 31
POOL_WIDTH = MIX_WIDTH - CONV_WIDTH
POOL_WINDOWS = (2, 4, 8, 16)
POOL_GROUPS = len(POOL_WINDOWS)
POOL_GROUP = POOL_WIDTH // POOL_GROUPS
ODD_COLS = 2 * CONV_WIDTH + POOL_WIDTH
XA_HEADS = 4
XA_HEAD = D_MODEL // XA_HEADS
N_EXPERTS = 32
TOP_K = 4
D_FF = D_MODEL // 2
SWIGLU_LIMIT = 7.0
SWIGLU_ALPHA = 1.702
MOE_BLOCK = 128

kernel_name = "hybrid_rwkv7_gdn_conformer_pool_moe_trunk"


def layer_norm(x, g, b, eps=LN_EPS):
    xf = x.astype(jnp.float32)
    mu = xf.mean(-1, keepdims=True)
    var = jnp.square(xf - mu).mean(-1, keepdims=True)
    return ((xf - mu) * lax.rsqrt(var + eps) * g + b).astype(x.dtype)


def l2_normalize(x, eps=1e-6):
    xf = x.astype(jnp.float32)
    return xf * lax.rsqrt(jnp.sum(xf * xf, -1, keepdims=True) + eps)


def split_heads(t, n_heads):
    return t.reshape(*t.shape[:-1], n_heads, t.shape[-1] // n_heads)


def shift_prev(x):
    return jnp.pad(x, ((0, 0), (1, 0), (0, 0)))[:, :-1]


def causal_depthwise_conv(x, w):
    width, ch = w.shape
    return lax.conv_general_dilated(
        x, w[:, None, :].astype(x.dtype), window_strides=(1,), padding=[(width - 1, 0)],
        dimension_numbers=('NWC', 'WIO', 'NWC'), feature_group_count=ch)


def rwkv7_scan(r, log_decay, k, v, kk, a):
    b, s, h, n = r.shape

    def step(state, inp):
        r_t, d_t, k_t, v_t, kk_t, a_t = inp
        sa = jnp.einsum('bhvk,bhk->bhv', state, -kk_t)
        state = (state * d_t[:, :, None, :] + sa[..., None] * (kk_t * a_t)[:, :, None, :]
                 + v_t[..., None] * k_t[:, :, None, :])
        return state, jnp.einsum('bhvk,bhk->bhv', state, r_t)

    xs = tuple(jnp.moveaxis(t, 1, 0) for t in (r, jnp.exp(log_decay), k, v, kk, a))
    _, y = lax.scan(step, jnp.zeros((b, h, n, n), jnp.float32), xs)
    return jnp.moveaxis(y, 0, 1)


def rwkv7_mix(p, mu, w0, w2, a0, a2, g2, k_k, k_a, r_k, gn_g, gn_b):
    b, s, _ = p.shape
    W, H = RWKV_WIDTH, RWKV_HEADS
    p = p.astype(jnp.float32)
    p = p + (shift_prev(p) - p) * mu
    r, k, v = p[..., :W], p[..., W:2 * W], p[..., 2 * W:3 * W]
    o1 = 3 * W
    o2 = o1 + RWKV_DECAY_RANK
    o3 = o2 + RWKV_ICLR_RANK
    w = w0 + jnp.tanh(p[..., o1:o2]) @ w2
    log_decay = -jnp.exp(-jax.nn.softplus(-w) - 0.5)
    a = jax.nn.sigmoid(a0 + p[..., o2:o3] @ a2)
    g = jax.nn.sigmoid(p[..., o3:]) @ g2
    r, k, v, log_decay, a = (split_heads(t, H) for t in (r, k, v, log_decay, a))
    kk = l2_normalize(k * split_heads(k_k, H))
    k = k * (1.0 + (a - 1.0) * split_heads(k_a, H))
    y = rwkv7_scan(r, log_decay, k, v, kk, a)
    mean = y.mean(-1, keepdims=True)
    var = jnp.square(y - mean).mean(-1, keepdims=True)
    y = ((y - mean) * lax.rsqrt(var + RWKV_GN_EPS)).reshape(b, s, W) * gn_g + gn_b
    bonus = jnp.sum(r * k * r_k, -1, keepdims=True) * v
    return (y + bonus.reshape(b, s, W)) * g


def chunked_gated_delta(q, k, v, log_alpha, beta):
    b, s, h, dk = q.shape
    nc = s // CHUNK

    def chunks(t):
        t = t.reshape(b, nc, CHUNK, h, *t.shape[3:])
        return jnp.moveaxis(t, (1, 2), (0, 3))

    qc, kc, vc, bc = chunks(q), chunks(k), chunks(v), chunks(beta)
    gc = jnp.cumsum(chunks(log_alpha), axis=-1)
    idx = jnp.arange(CHUNK)
    causal = idx[:, None] >= idx[None, :]
    strict = idx[:, None] > idx[None, :]
    diff = gc[..., :, None] - gc[..., None, :]
    decay = jnp.where(causal, jnp.exp(jnp.where(causal, diff, 0.0)), 0.0)
    kb = kc * bc[..., None]
    lower = jnp.where(strict, jnp.einsum('nbhid,nbhjd->nbhij', kb, kc) * decay, 0.0)
    eye = jnp.eye(CHUNK, dtype=jnp.float32)
    tinv = lax.linalg.triangular_solve(eye + lower, jnp.broadcast_to(eye, lower.shape),
                                       left_side=True, lower=True)
    u = tinv @ (vc * bc[..., None])
    w = tinv @ (kb * jnp.exp(gc)[..., None])
    qk = jnp.einsum('nbhid,nbhjd->nbhij', qc, kc) * decay
    q_dec = qc * jnp.exp(gc)[..., None]
    k_dec = kc * jnp.exp(gc[..., -1:] - gc)[..., None]
    g_last = jnp.exp(gc[..., -1])[..., None, None]

    def step(state, inp):
        u_i, w_i, q_i, k_i, qk_i, gl_i = inp
        v_new = u_i - w_i @ state
        out = q_i @ state + qk_i @ v_new
        state = state * gl_i + jnp.swapaxes(k_i, -1, -2) @ v_new
        return state, out

    s0 = jnp.zeros((b, h, dk, v.shape[-1]), jnp.float32)
    _, o = lax.scan(step, s0, (u, w, q_dec, k_dec, qk, g_last))
    return jnp.moveaxis(o, (0, 3), (1, 2)).reshape(b, s, h, -1)


def gated_deltanet_mix(p, conv_w, a_log, dt_bias, norm_g):
    b, s, _ = p.shape
    W, H = GDN_WIDTH, GDN_HEADS
    qkv = jax.nn.silu(causal_depthwise_conv(p[..., :3 * W], conv_w)).astype(jnp.float32)
    q = l2_normalize(split_heads(qkv[..., :W], H)) * GDN_HEAD ** -0.5
    k = l2_normalize(split_heads(qkv[..., W:2 * W], H))
    v = split_heads(qkv[..., 2 * W:], H)
    gate = p[..., 3 * W:4 * W].astype(jnp.float32)
    beta = jax.nn.sigmoid(p[..., 4 * W:4 * W + H].astype(jnp.float32))
    log_alpha = -jnp.exp(a_log.astype(jnp.float32)) * jax.nn.softplus(
        p[..., 4 * W + H:].astype(jnp.float32) + dt_bias)
    o = chunked_gated_delta(q, k, v, log_alpha, beta)
    o = o * lax.rsqrt(jnp.mean(o * o, -1, keepdims=True) + 1e-6) * norm_g
    return o.reshape(b, s, W) * jax.nn.silu(gate)


def even_mixer(x, w_in, mu, w0, w2, a0, a2, g2, k_k, k_a, r_k, gn_g, gn_b,
               conv_w, a_log, dt_bias, norm_g, w_out):
    p = x @ w_in
    ya = rwkv7_mix(p[..., :RWKV_COLS], mu, w0, w2, a0, a2, g2, k_k, k_a, r_k, gn_g, gn_b)
    yb = gated_deltanet_mix(p[..., RWKV_COLS:], conv_w, a_log, dt_bias, norm_g)
    return jnp.concatenate([ya, yb], -1).astype(x.dtype) @ w_out


def odd_mixer(x, w_in, cv_w, cv_b, cv_ln_g, cv_ln_b, pl_w, pl_scale, w_out):
    b, s, _ = x.shape
    p = x @ w_in
    C = CONV_WIDTH
    u = p[..., :C] * jax.nn.sigmoid(p[..., C:2 * C])
    u = causal_depthwise_conv(u, cv_w) + cv_b
    u = jax.nn.silu(layer_norm(u, cv_ln_g, cv_ln_b).astype(jnp.float32))
    z = p[..., 2 * C:].astype(jnp.float32).reshape(b, s, POOL_GROUPS, POOL_GROUP)
    cs = jnp.cumsum(z, axis=1)
    t1 = jnp.arange(1, s + 1, dtype=jnp.float32)
    pooled = []
    for gi, win in enumerate(POOL_WINDOWS):
        c = cs[:, :, gi]
        lagged = jnp.pad(c, ((0, 0), (win, 0), (0, 0)))[:, :s]
        pooled.append((c - lagged) / jnp.minimum(t1, win)[None, :, None])
    pooled = jnp.stack(pooled, 2) - z
    mixed = jnp.einsum('bsgc,gcd->bsgd', pooled, pl_w).reshape(b, s, POOL_WIDTH) * pl_scale
    return jnp.concatenate([u, mixed], -1).astype(x.dtype) @ w_out


def cross_attention(x, mem, wq, wk, wv, wo):
    b, s, d = x.shape
    q = split_heads(x @ wq, XA_HEADS)
    k = split_heads(mem @ wk, XA_HEADS)
    v = split_heads(mem @ wv, XA_HEADS)
    scores = jnp.einsum('bshd,bmhd->bhsm', q, k).astype(jnp.float32) * XA_HEAD ** -0.5
    probs = jax.nn.softmax(scores, axis=-1).astype(v.dtype)
    o = jnp.einsum('bhsm,bmhd->bshd', probs, v).reshape(b, s, d)
    return o @ wo


def clamped_swiglu(h):
    gate, up = h[..., ::2], h[..., 1::2]
    gate = jnp.minimum(gate, SWIGLU_LIMIT)
    up = jnp.clip(up, -SWIGLU_LIMIT, SWIGLU_LIMIT)
    return (up + 1.0) * gate * jax.nn.sigmoid(SWIGLU_ALPHA * gate)


def moe_ffn(x, w_r, b_r, w_gu, b_gu, w_dn, b_dn):
    b, s, d = x.shape
    n = b * s
    n_assign = n * TOP_K
    xf = x.reshape(n, d)
    logits = (xf @ w_r + b_r).astype(jnp.float32)
    top_logits, top_idx = lax.top_k(logits, TOP_K)
    gates = jax.nn.softmax(top_logits, axis=-1)
    expert = top_idx.reshape(-1)
    token = jnp.repeat(jnp.arange(n, dtype=jnp.int32), TOP_K)
    gate = gates.reshape(-1)
    order = jnp.argsort(expert)
    expert, token, gate = expert[order], token[order], gate[order]
    counts = jnp.bincount(expert, length=N_EXPERTS)
    starts = jnp.cumsum(counts) - counts
    padded = -(-counts // MOE_BLOCK) * MOE_BLOCK
    pad_end = jnp.cumsum(padded)
    pad_start = pad_end - padded
    dest = pad_start[expert] + jnp.arange(n_assign) - starts[expert]
    cap = n_assign + N_EXPERTS * MOE_BLOCK
    n_blocks = cap // MOE_BLOCK
    row_token = jnp.zeros((cap,), jnp.int32).at[dest].set(token)
    row_gate = jnp.zeros((cap,), jnp.float32).at[dest].set(gate)
    block_expert = jnp.minimum(
        jnp.searchsorted(pad_end, jnp.arange(n_blocks) * MOE_BLOCK, side='right'), N_EXPERTS - 1)

    def expert_block(args):
        tok, e = args
        h = xf[tok] @ w_gu[e] + b_gu[e]
        return clamped_swiglu(h) @ w_dn[e] + b_dn[e]

    rows = lax.map(expert_block, (row_token.reshape(n_blocks, MOE_BLOCK), block_expert))
    rows = rows.reshape(cap, d) * row_gate[:, None].astype(rows.dtype)
    return jnp.zeros_like(xf).at[row_token].add(rows).reshape(b, s, d)


def setup_inputs(seed: int = 0) -> dict:
    key = jax.random.key(seed)
    ks = iter(jax.random.split(key, 48))

    def nrm(shape, scale):
        return scale * jax.random.normal(next(ks), shape, jnp.float32)

    def unif(shape, lo, hi):
        return jax.random.uniform(next(ks), shape, jnp.float32, lo, hi)

    D, NE, NO, L = D_MODEL, N_EVEN, N_ODD, DEPTH
    dt = jnp.exp(unif((NE, GDN_HEADS), math.log(1e-3), math.log(1e-1)))
    return {
        "x": nrm((BATCH, SEQ, D), 1.0),
        "mem": nrm((BATCH, MEM_LEN, D), 1.0),
        "ev_w_in": nrm((NE, D, EVEN_COLS), D ** -0.5),
        "ev_mu": unif((NE, RWKV_COLS), 0.0, 1.0),
        "rk_w0": unif((NE, RWKV_WIDTH), -6.5, -1.5),
        "rk_w2": nrm((NE, RWKV_DECAY_RANK, RWKV_WIDTH), 0.1 * RWKV_DECAY_RANK ** -0.5),
        "rk_a0": nrm((NE, RWKV_WIDTH), 0.1),
        "rk_a2": nrm((NE, RWKV_ICLR_RANK, RWKV_WIDTH), 0.1 * RWKV_ICLR_RANK ** -0.5),
        "rk_g2": nrm((NE, RWKV_GATE_RANK, RWKV_WIDTH), RWKV_GATE_RANK ** -0.5),
        "rk_kk": 0.85 + nrm((NE, RWKV_WIDTH), 0.05),
        "rk_ka": 1.0 + nrm((NE, RWKV_WIDTH), 0.05),
        "rk_rk": nrm((NE, RWKV_HEADS, RWKV_HEAD), 0.1),
        "rk_gn_g": 1.0 + nrm((NE, RWKV_WIDTH), 0.1),
        "rk_gn_b": nrm((NE, RWKV_WIDTH), 0.02),
        "gd_conv": nrm((NE, GDN_CONV, 3 * GDN_WIDTH), GDN_CONV ** -0.5),
        "gd_a_log": jnp.log(unif((NE, GDN_HEADS), 1.0, 16.0)),
        "gd_dt_bias": dt + jnp.log(-jnp.expm1(-dt)),
        "gd_norm_g": 1.0 + nrm((NE, GDN_HEAD), 0.1),
        "ev_w_out": nrm((NE, MIX_WIDTH, D), MIX_WIDTH ** -0.5 * DEEPNORM_BETA),
        "od_w_in": nrm((NO, D, ODD_COLS), D ** -0.5),
        "cv_w": nrm((NO, CONV_KERNEL, CONV_WIDTH), CONV_KERNEL ** -0.5),
        "cv_b": nrm((NO, CONV_WIDTH), 0.02),
        "cv_ln_g": 1.0 + nrm((NO, CONV_WIDTH), 0.1),
        "cv_ln_b": nrm((NO, CONV_WIDTH), 0.02),
        "pl_w": nrm((NO, POOL_GROUPS, POOL_GROUP, POOL_GROUP), POOL_GROUP ** -0.5),
        "pl_scale": 1.0 + nrm((NO, POOL_WIDTH), 0.1),
        "od_w_out": nrm((NO, MIX_WIDTH, D), MIX_WIDTH ** -0.5 * DEEPNORM_BETA),
        "xa_wq": nrm((L, D, D), D ** -0.5),
        "xa_wk": nrm((L, D, D), D ** -0.5),
        "xa_wv": nrm((L, D, D), D ** -0.5 * DEEPNORM_BETA),
        "xa_wo": nrm((L, D, D), D ** -0.5 * DEEPNORM_BETA),
        "moe_wr": nrm((L, D, N_EXPERTS), D ** -0.5),
        "moe_br": nrm((L, N_EXPERTS), 0.01),
        "moe_wgu": nrm((L, N_EXPERTS, D, 2 * D_FF), D ** -0.5),
        "moe_bgu": nrm((L, N_EXPERTS, 2 * D_FF), 0.02),
        "moe_wdn": nrm((L, N_EXPERTS, D_FF, D), D_FF ** -0.5 * DEEPNORM_BETA),
        "moe_bdn": nrm((L, N_EXPERTS, D), 0.02),
        "ln_g": 1.0 + nrm((L, 3, D), 0.1),
        "ln_b": nrm((L, 3, D), 0.02),
    }


def reference(x, mem, ev_w_in, ev_mu, rk_w0, rk_w2, rk_a0, rk_a2, rk_g2, rk_kk, rk_ka, rk_rk,
              rk_gn_g, rk_gn_b, gd_conv, gd_a_log, gd_dt_bias, gd_norm_g, ev_w_out,
              od_w_in, cv_w, cv_b, cv_ln_g, cv_ln_b, pl_w, pl_scale, od_w_out,
              xa_wq, xa_wk, xa_wv, xa_wo, moe_wr, moe_br, moe_wgu, moe_bgu, moe_wdn, moe_bdn,
              ln_g, ln_b):
    for layer in range(DEPTH):
        i = layer // 2
        if layer % 2 == 0:
            h = even_mixer(x, ev_w_in[i], ev_mu[i], rk_w0[i], rk_w2[i], rk_a0[i], rk_a2[i],
                           rk_g2[i], rk_kk[i], rk_ka[i], rk_rk[i], rk_gn_g[i], rk_gn_b[i],
                           gd_conv[i], gd_a_log[i], gd_dt_bias[i], gd_norm_g[i], ev_w_out[i])
        else:
            h = odd_mixer(x, od_w_in[i], cv_w[i], cv_b[i], cv_ln_g[i], cv_ln_b[i],
                          pl_w[i], pl_scale[i], od_w_out[i])
        x = layer_norm(DEEPNORM_ALPHA * x + h, ln_g[layer, 0], ln_b[layer, 0])
        a = cross_attention(x, mem, xa_wq[layer], xa_wk[layer], xa_wv[layer], xa_wo[layer])
        x = layer_norm(DEEPNORM_ALPHA * x + a, ln_g[layer, 1], ln_b[layer, 1])
        f = moe_ffn(x, moe_wr[layer], moe_br[layer], moe_wgu[layer], moe_bgu[layer],
                    moe_wdn[layer], moe_bdn[layer])
        x = layer_norm(DEEPNORM_ALPHA * x + f, ln_g[layer, 2], ln_b[layer, 2])
    return x
```

```python
import functools
import math

import jax
import jax.numpy as jnp
from jax import lax
from jax.experimental import pallas as pl
from jax.experimental.pallas import tpu as pltpu

F32 = jnp.float32
BF16 = jnp.bfloat16

LANES = 128
VMEM_LIMIT = 56 * 1024 * 1024

DEPTH = 2
DEEPNORM_ALPHA = (2 * DEPTH) ** 0.25
LN_EPS = 1e-5
CHUNK = 64
RWKV_HEAD = 64
RWKV_GN_EPS = 64e-5
GDN_HEAD = 128
GDN_CONV = 4
CONV_KERNEL = 31
POOL_WINDOWS = (2, 4, 8, 16)
XA_HEADS = 4
N_EXPERTS = 32
TOP_K = 4
SWIGLU_LIMIT = 7.0
SWIGLU_ALPHA = 1.702
MOE_ROWS = 256

NN = (((1,), (0,)), ((), ()))
NT = (((1,), (1,)), ((), ()))


def _cparams(sem):
    return pltpu.CompilerParams(dimension_semantics=sem, vmem_limit_bytes=VMEM_LIMIT)


def _bdot(a, b, dims=NN):
    return lax.dot_general(a.astype(BF16), b.astype(BF16), dims, preferred_element_type=F32)


def _parts(x, n):
    out, rem = [], x
    for i in range(n):
        h = rem.astype(BF16)
        out.append(h)
        if i + 1 < n:
            rem = rem - h.astype(F32)
    return out


def _dotx(a, b, na=2, nb=2, dims=NN):
    ap, bp = _parts(a, na), _parts(b, nb)
    acc = None
    for i, ai in enumerate(ap):
        for j, bj in enumerate(bp):
            if i + j < max(na, nb):
                t = lax.dot_general(ai, bj, dims, preferred_element_type=F32)
                acc = t if acc is None else acc + t
    return acc


def _layer_norm(v, g, b):
    mu = jnp.mean(v, axis=-1, keepdims=True)
    c = v - mu
    var = jnp.mean(c * c, axis=-1, keepdims=True)
    return c * lax.rsqrt(var + LN_EPS) * g + b


def _sigmoid(x):
    return jax.nn.sigmoid(x)


def _silu(x):
    return x * jax.nn.sigmoid(x)


def _const_spec(shape):
    return pl.BlockSpec(shape, lambda *_: (0,) * len(shape), pipeline_mode=pl.Buffered(1))


def _mm_kernel(x_ref, w_ref, o_ref):
    o_ref[...] = _bdot(x_ref[...], w_ref[...])


def matmul(x, w, tm=1024, tn=512):
    m, k = x.shape
    n = w.shape[1]
    tm = min(tm, m)
    return pl.pallas_call(
        _mm_kernel,
        out_shape=jax.ShapeDtypeStruct((m, n), F32),
        grid=(pl.cdiv(m, tm), pl.cdiv(n, tn)),
        in_specs=[pl.BlockSpec((tm, k), lambda i, j: (i, 0)),
                  pl.BlockSpec((k, tn), lambda i, j: (0, j))],
        out_specs=pl.BlockSpec((tm, tn), lambda i, j: (i, j)),
        compiler_params=_cparams(("parallel", "arbitrary")),
        name="dense_matmul",
    )(x, w)


def _mm_res_ln_kernel(n_in, *refs):
    a_refs = refs[:n_in]
    w_refs = refs[n_in:2 * n_in]
    x_ref, g_ref, b_ref, o_ref = refs[2 * n_in:]
    h = None
    for a_ref, w_ref in zip(a_refs, w_refs):
        t = _bdot(a_ref[...], w_ref[...])
        h = t if h is None else h + t
    o_ref[...] = _layer_norm(DEEPNORM_ALPHA * x_ref[...] + h, g_ref[...], b_ref[...])


def mm_res_ln(a_list, w, x, g, b, tm=256):
    m, d = x.shape
    tm = min(tm, m)
    in_specs, off = [], 0
    for a in a_list:
        in_specs.append(pl.BlockSpec((tm, a.shape[1]), lambda i: (i, 0)))
    w_parts = []
    for a in a_list:
        ka = a.shape[1]
        w_parts.append(lax.slice_in_dim(w, off, off + ka, axis=0))
        in_specs.append(_const_spec((ka, d)))
        off += ka
    in_specs += [pl.BlockSpec((tm, d), lambda i: (i, 0)), _const_spec((1, d)), _const_spec((1, d))]
    return pl.pallas_call(
        functools.partial(_mm_res_ln_kernel, len(a_list)),
        out_shape=jax.ShapeDtypeStruct((m, d), F32),
        grid=(m // tm,),
        in_specs=in_specs,
        out_specs=pl.BlockSpec((tm, d), lambda i: (i, 0)),
        compiler_params=_cparams(("parallel",)),
        name="proj_residual_layernorm",
    )(*a_list, *w_parts, x, g.reshape(1, d), b.reshape(1, d))


def _group_sum(x, gmat):
    return _dotx(x, gmat, 2, 1)


def _rwkv_kernel(pr_ref, pk_ref, pv_ref, pl_ref, mur_ref, muk_ref, muv_ref, mul_ref,
                 w0_ref, w2_ref, a0_ref, a2_ref, g2_ref, kk_ref, ka_ref, rk_ref, gng_ref, gnb_ref,
                 o_ref, prev_ref, prevl_ref, state_ref):
    c = pl.program_id(1)
    nb, ch, _ = pr_ref.shape

    @pl.when(c == 0)
    def _():
        prev_ref[...] = jnp.zeros_like(prev_ref)
        prevl_ref[...] = jnp.zeros_like(prevl_ref)
        state_ref[...] = jnp.zeros_like(state_ref)

    col = lax.broadcasted_iota(jnp.int32, (ch, LANES), 1)
    mlo = (col < RWKV_HEAD).astype(F32)
    mhi = 1.0 - mlo
    r2 = lax.broadcasted_iota(jnp.int32, (2 * ch, 2 * ch), 0)
    c2 = lax.broadcasted_iota(jnp.int32, (2 * ch, 2 * ch), 1)
    same = (r2 // ch) == (c2 // ch)
    strict = same & ((r2 % ch) > (c2 % ch))
    incl = same & ((r2 % ch) >= (c2 % ch))
    eye = (r2 == c2).astype(F32)
    gmat = ((r2 // RWKV_HEAD) == (c2 // RWKV_HEAD)).astype(BF16)
    rr = lax.broadcasted_iota(jnp.int32, (ch, ch), 0)
    cc = lax.broadcasted_iota(jnp.int32, (ch, ch), 1)
    tri = (rr >= cc).astype(BF16)

    def shifted(x, prev):
        return jnp.where(lax.broadcasted_iota(jnp.int32, x.shape, 0) == 0, prev, pltpu.roll(x, 1, 0))

    def stack(x):
        return jnp.concatenate([x * mlo, x * mhi], axis=0)

    for b in range(nb):
        raw = [pr_ref[b], pk_ref[b], pv_ref[b]]
        rawl = pl_ref[b]
        mus = [mur_ref[...], muk_ref[...], muv_ref[...]]
        mixed = []
        for i in range(3):
            xs = shifted(raw[i], prev_ref[b, i])
            mixed.append(raw[i] + (xs - raw[i]) * mus[i])
        xsl = shifted(rawl, prevl_ref[b])
        lora = rawl + (xsl - rawl) * mul_ref[...]
        for i in range(3):
            prev_ref[b, i] = raw[i][ch - 1:ch]
        prevl_ref[b] = rawl[ch - 1:ch]

        r, k, v = mixed
        l1, l2 = lora[:, :LANES], lora[:, LANES:]
        w = w0_ref[...] + _dotx(jnp.tanh(l1), w2_ref[...])
        ld = -math.exp(-0.5) * _sigmoid(w)
        a = _sigmoid(a0_ref[...] + _dotx(l1, a2_ref[...]))
        g = _dotx(_sigmoid(l2), g2_ref[...])
        kraw = k * kk_ref[...]
        kk = kraw * lax.rsqrt(_group_sum(kraw * kraw, gmat) + 1e-6)
        kmod = k * (1.0 + (a - 1.0) * ka_ref[...])

        cum = _dotx(tri, ld, 1, 3)
        w_t = jnp.exp(cum)
        w_prev = jnp.exp(cum - ld)
        w_inv = jnp.exp(-cum)
        w_end = jnp.exp(cum[ch - 1:ch])
        beta = kk * a
        a2 = stack(-kk * w_prev)
        b2 = stack(beta * w_inv)
        k2 = stack(kmod * w_inv)
        q2 = stack(r * w_t)
        v2 = stack(v)
        bh2 = stack(beta * w_inv * w_end)
        kh2 = stack(kmod * w_inv * w_end)

        lab = jnp.where(strict, _dotx(a2, b2, dims=NT), 0.0)
        lak = jnp.where(strict, _dotx(a2, k2, dims=NT), 0.0)
        grb = jnp.where(incl, _dotx(q2, b2, dims=NT), 0.0)
        grk = jnp.where(incl, _dotx(q2, k2, dims=NT), 0.0)
        minv = eye + lab
        pw = lab
        for _ in range(5):
            pw = _dotx(pw, pw)
            minv = minv + _dotx(minv, pw)
        t0 = state_ref[b]
        u2 = _dotx(minv, _dotx(a2, t0) + _dotx(lak, v2))
        y2 = _dotx(q2, t0) + _dotx(grb, u2) + _dotx(grk, v2)
        state_ref[b] = w_end.T * t0 + _dotx(bh2.T, u2) + _dotx(kh2.T, v2)
        y = y2[:ch] + y2[ch:]

        inv_n = 1.0 / RWKV_HEAD
        mean = _group_sum(y, gmat) * inv_n
        yc = y - mean
        var = _group_sum(yc * yc, gmat) * inv_n
        yn = yc * lax.rsqrt(var + RWKV_GN_EPS) * gng_ref[...] + gnb_ref[...]
        bonus = _group_sum(r * kmod * rk_ref[...], gmat) * v
        o_ref[b] = (yn + bonus) * g


def rwkv_mix(p3, mu, w0, w2, a0, a2, g2, k_k, k_a, r_k, gn_g, gn_b):
    nb, s, _ = p3.shape
    width = w0.shape[0]
    npair = width // LANES
    nlb = 3 * npair
    rank_w, rank_a = w2.shape[0], a2.shape[0]
    assert rank_w + rank_a == LANES and g2.shape[0] == LANES and 2 * CHUNK == LANES
    w2p = jnp.concatenate([w2, jnp.zeros((rank_a, width), F32)], 0)
    a2p = jnp.concatenate([jnp.zeros((rank_w, width), F32), a2], 0)
    row = lambda t: t.reshape(1, -1)
    blk = lambda off: pl.BlockSpec((nb, CHUNK, LANES), lambda j, c: (0, c, off + j))
    vec = lambda off: pl.BlockSpec((1, LANES), lambda j, c: (0, off + j))
    mat = pl.BlockSpec((LANES, LANES), lambda j, c: (0, j))
    mu2 = row(mu)
    return pl.pallas_call(
        _rwkv_kernel,
        out_shape=jax.ShapeDtypeStruct((nb, s, width), F32),
        grid=(npair, s // CHUNK),
        in_specs=[blk(0), blk(npair), blk(2 * npair),
                  pl.BlockSpec((nb, CHUNK, 2 * LANES), lambda j, c: (0, c, nlb // 2)),
                  vec(0), vec(npair), vec(2 * npair),
                  pl.BlockSpec((1, 2 * LANES), lambda j, c: (0, nlb // 2)),
                  vec(0), mat, vec(0), mat, mat, vec(0), vec(0), vec(0), vec(0), vec(0)],
        out_specs=pl.BlockSpec((nb, CHUNK, LANES), lambda j, c: (0, c, j)),
        scratch_shapes=[pltpu.VMEM((nb, 3, 1, LANES), F32),
                        pltpu.VMEM((nb, 1, 2 * LANES), F32),
                        pltpu.VMEM((nb, LANES, LANES), F32)],
        compiler_params=_cparams(("parallel", "arbitrary")),
        name="rwkv7_mixer",
    )(p3, p3, p3, p3, mu2, mu2, mu2, mu2, row(w0), w2p, row(a0), a2p, g2,
      row(k_k), row(k_a), row(r_k), row(gn_g), row(gn_b))


def _gdn_kernel(pq_ref, pk_ref, pv_ref, pg_ref, pbd_ref, cq_ref, ck_ref, cv_ref, alog_ref, dtb_ref,
                ng_ref, o_ref, halo_ref, state_ref):
    h = pl.program_id(0)
    c = pl.program_id(1)
    nb, ch, _ = pq_ref.shape
    nheads = alog_ref.shape[1]

    @pl.when(c == 0)
    def _():
        halo_ref[...] = jnp.zeros_like(halo_ref)
        state_ref[...] = jnp.zeros_like(state_ref)

    col = lax.broadcasted_iota(jnp.int32, (ch, LANES), 1)
    rr = lax.broadcasted_iota(jnp.int32, (ch, ch), 0)
    cc = lax.broadcasted_iota(jnp.int32, (ch, ch), 1)
    causal = rr >= cc
    strict = rr > cc
    tri = causal.astype(BF16)
    triu = (rr <= cc).astype(BF16)
    eye = (rr == cc).astype(F32)
    ones = jnp.ones((LANES, LANES), BF16)
    hcol = lax.broadcasted_iota(jnp.int32, (1, nheads), 1)
    a_coef = -jnp.exp(jnp.sum(jnp.where(hcol == h, alog_ref[...], 0.0), axis=-1, keepdims=True))
    dt_b = jnp.sum(jnp.where(hcol == h, dtb_ref[...], 0.0), axis=-1, keepdims=True)

    def conv_silu(x_ref, w_ref, b, i):
        halo_ref[b, i, 8:, :] = x_ref[b]
        acc = None
        for j in range(GDN_CONV):
            t = halo_ref[b, i, 8 - (GDN_CONV - 1) + j:8 - (GDN_CONV - 1) + j + ch, :] * w_ref[j:j + 1, :]
            acc = t if acc is None else acc + t
        halo_ref[b, i, 0:8, :] = halo_ref[b, i, ch:ch + 8, :]
        return _silu(acc)

    for b in range(nb):
        q = conv_silu(pq_ref, cq_ref, b, 0)
        k = conv_silu(pk_ref, ck_ref, b, 1)
        v = conv_silu(pv_ref, cv_ref, b, 2)
        q = q * lax.rsqrt(_dotx(q * q, ones, 2, 1) + 1e-6) * (GDN_HEAD ** -0.5)
        k = k * lax.rsqrt(_dotx(k * k, ones, 2, 1) + 1e-6)
        bd = pbd_ref[b]
        bcol = jnp.sum(jnp.where(col == h, bd, 0.0), axis=-1, keepdims=True)
        dcol = jnp.sum(jnp.where(col == h + nheads, bd, 0.0), axis=-1, keepdims=True)
        beta = _sigmoid(bcol)
        z = dcol + dt_b
        softplus = jnp.maximum(z, 0.0) + jnp.log(1.0 + jnp.exp(-jnp.abs(z)))
        la = jnp.broadcast_to(a_coef * softplus, (ch, LANES))
        gc = _dotx(tri, la, 1, 3)
        la_rows = jnp.broadcast_to(la.T[0:1, :], (ch, ch))
        gc_row = _dotx(la_rows, triu, 3, 1)
        diff = gc[:, :ch] - gc_row
        decay = jnp.where(causal, jnp.exp(jnp.where(causal, diff, 0.0)), 0.0)
        eg = jnp.exp(gc)
        g_last = eg[ch - 1:ch, :]
        kb = k * beta
        lower = jnp.where(strict, _dotx(kb, k, dims=NT) * decay, 0.0)
        tinv = eye - lower
        pw = -lower
        for _ in range(5):
            pw = _dotx(pw, pw)
            tinv = tinv + _dotx(tinv, pw)
        u = _dotx(tinv, v * beta)
        w = _dotx(tinv, kb * eg)
        qk = _dotx(q, k, dims=NT) * decay
        q_dec = q * eg
        k_dec = k * jnp.exp(gc[ch - 1:ch, :] - gc)
        t0 = state_ref[b]
        v_new = u - _dotx(w, t0)
        out = _dotx(q_dec, t0) + _dotx(qk, v_new)
        state_ref[b] = t0 * g_last[:, 0:1] + _dotx(k_dec.T, v_new)
        ms = _dotx(out * out, ones, 2, 1) * (1.0 / GDN_HEAD)
        o = out * lax.rsqrt(ms + 1e-6) * ng_ref[...]
        o_ref[b] = o * _silu(pg_ref[b])


def gdn_mix(p3, col0, conv_w, a_log, dt_bias, norm_g):
    nb, s, _ = p3.shape
    nheads = a_log.shape[0]
    width = nheads * GDN_HEAD
    assert col0 % LANES == 0 and GDN_HEAD == LANES
    o0 = col0 // LANES
    blk = lambda off: pl.BlockSpec((nb, CHUNK, LANES), lambda h, c: (0, c, off + h))
    cw = lambda off: pl.BlockSpec((GDN_CONV, LANES), lambda h, c: (0, off + h))
    return pl.pallas_call(
        _gdn_kernel,
        out_shape=jax.ShapeDtypeStruct((nb, s, width), F32),
        grid=(nheads, s // CHUNK),
        in_specs=[blk(o0), blk(o0 + nheads), blk(o0 + 2 * nheads), blk(o0 + 3 * nheads),
                  pl.BlockSpec((nb, CHUNK, LANES), lambda h, c: (0, c, o0 + 4 * nheads)),
                  cw(0), cw(nheads), cw(2 * nheads),
                  pl.BlockSpec((1, nheads), lambda h, c: (0, 0)),
                  pl.BlockSpec((1, nheads), lambda h, c: (0, 0)),
                  pl.BlockSpec((1, LANES), lambda h, c: (0, 0))],
        out_specs=pl.BlockSpec((nb, CHUNK, LANES), lambda h, c: (0, c, h)),
        scratch_shapes=[pltpu.VMEM((nb, 3, 8 + CHUNK, LANES), F32),
                        pltpu.VMEM((nb, LANES, LANES), F32)],
        compiler_params=_cparams(("parallel", "arbitrary")),
        name="gated_deltanet_mixer",
    )(p3, p3, p3, p3, p3, conv_w, conv_w, conv_w, a_log.reshape(1, -1), dt_bias.reshape(1, -1),
      norm_g.reshape(1, -1))


HALO_U = 32
HALO_Z = 16


def _odd_kernel(p_ref, cvw_ref, cvb_ref, lng_ref, lnb_ref, plw_ref, pls_ref, o_ref, ubuf_ref, zbuf_ref):
    s_idx = pl.program_id(1)
    ts = p_ref.shape[1]
    cw = cvb_ref.shape[1]
    pool_w = pls_ref.shape[1]
    pg = pool_w // len(POOL_WINDOWS)

    @pl.when(s_idx == 0)
    def _():
        ubuf_ref[0:HALO_U, :] = jnp.zeros((HALO_U, cw), F32)
        zbuf_ref[0:HALO_Z, :] = jnp.zeros((HALO_Z, pool_w), F32)

    pa = p_ref[0, :, 0:cw]
    pb = p_ref[0, :, cw:2 * cw]
    ubuf_ref[HALO_U:, :] = pa * _sigmoid(pb)
    acc = jnp.broadcast_to(cvb_ref[...], (ts, cw))
    base = HALO_U - (CONV_KERNEL - 1)
    for j in range(CONV_KERNEL):
        acc = acc + ubuf_ref[base + j:base + j + ts, :] * cvw_ref[j:j + 1, :]
    ubuf_ref[0:HALO_U, :] = ubuf_ref[ts:ts + HALO_U, :]
    o_ref[0, :, 0:cw] = _silu(_layer_norm(acc, lng_ref[...], lnb_ref[...]))

    z = p_ref[0, :, 2 * cw:]
    zbuf_ref[HALO_Z:, :] = z
    t1 = (s_idx * ts + 1 + lax.broadcasted_iota(jnp.int32, (ts, 1), 0)).astype(F32)
    for gi, win in enumerate(POOL_WINDOWS):
        lo = gi * pg
        ssum = None
        for j in range(win):
            t = zbuf_ref[HALO_Z - j:HALO_Z - j + ts, lo:lo + pg]
            ssum = t if ssum is None else ssum + t
        pooled = ssum / jnp.minimum(t1, float(win)) - z[:, lo:lo + pg]
        mixed = _bdot(pooled, plw_ref[gi]) * pls_ref[:, lo:lo + pg]
        o_ref[0, :, cw + lo:cw + lo + pg] = mixed
    zbuf_ref[0:HALO_Z, :] = zbuf_ref[ts:ts + HALO_Z, :]


def odd_mix(p3, cv_w, cv_b, cv_ln_g, cv_ln_b, pl_w, pl_scale, ts=256):
    nb, s, cols = p3.shape
    cw = cv_b.shape[0]
    pool_w = pl_scale.shape[0]
    ts = min(ts, s)
    row = lambda t: t.reshape(1, -1)
    cvw = jnp.concatenate([cv_w, jnp.zeros((HALO_U - CONV_KERNEL, cw), F32)], 0)
    return pl.pallas_call(
        _odd_kernel,
        out_shape=jax.ShapeDtypeStruct((nb, s, cw + pool_w), F32),
        grid=(nb, s // ts),
        in_specs=[pl.BlockSpec((1, ts, cols), lambda b, i: (b, i, 0)),
                  _const_spec((HALO_U, cw)), _const_spec((1, cw)), _const_spec((1, cw)), _const_spec((1, cw)),
                  _const_spec(pl_w.shape), _const_spec((1, pool_w))],
        out_specs=pl.BlockSpec((1, ts, cw + pool_w), lambda b, i: (b, i, 0)),
        scratch_shapes=[pltpu.VMEM((HALO_U + ts, cw), F32), pltpu.VMEM((HALO_Z + ts, pool_w), F32)],
        compiler_params=_cparams(("parallel", "arbitrary")),
        name="conformer_pool_mixer",
    )(p3, cvw, row(cv_b), row(cv_ln_g), row(cv_ln_b), pl_w.astype(BF16), row(pl_scale))


def _xattn_kernel(x_ref, k_ref, v_ref, wq_ref, wo_ref, g_ref, b_ref, o_ref):
    x = x_ref[0]
    d = x.shape[1]
    dh = d // XA_HEADS
    q = _bdot(x, wq_ref[...])
    heads = []
    for hd in range(XA_HEADS):
        sl = slice(hd * dh, (hd + 1) * dh)
        s = _bdot(q[:, sl], k_ref[0, :, sl], NT) * (dh ** -0.5)
        s = s - jnp.max(s, axis=-1, keepdims=True)
        e = jnp.exp(s)
        prob = e / jnp.sum(e, axis=-1, keepdims=True)
        heads.append(_bdot(prob, v_ref[0, :, sl]))
    a = _bdot(jnp.concatenate(heads, axis=-1), wo_ref[...])
    o_ref[0] = _layer_norm(DEEPNORM_ALPHA * x + a, g_ref[...], b_ref[...])


def cross_attention_ln(x3, kv3, wq, wo, g, b, ts=256):
    nb, s, d = x3.shape
    m = kv3.shape[1]
    ts = min(ts, s)
    return pl.pallas_call(
        _xattn_kernel,
        out_shape=jax.ShapeDtypeStruct((nb, s, d), F32),
        grid=(nb, s // ts),
        in_specs=[pl.BlockSpec((1, ts, d), lambda bi, i: (bi, i, 0)),
                  pl.BlockSpec((1, m, d), lambda bi, i: (bi, 0, 0)),
                  pl.BlockSpec((1, m, d), lambda bi, i: (bi, 0, 1)),
                  _const_spec((d, d)), _const_spec((d, d)), _const_spec((1, d)), _const_spec((1, d))],
        out_specs=pl.BlockSpec((1, ts, d), lambda bi, i: (bi, i, 0)),
        compiler_params=_cparams(("parallel", "parallel")),
        name="cross_attention_layernorm",
    )(x3, kv3, kv3, wq, wo, g.reshape(1, d), b.reshape(1, d))


def _router_kernel(x_ref, wr_ref, br_ref, idx_ref, gate_ref, rank_ref, cnt_ref, base_ref, tri_ref):
    i = pl.program_id(0)
    tm = x_ref.shape[0]
    ne = wr_ref.shape[0]

    @pl.when(i == 0)
    def _():
        base_ref[...] = jnp.zeros_like(base_ref)
        rr = lax.broadcasted_iota(jnp.int32, (tm, tm), 0)
        cc = lax.broadcasted_iota(jnp.int32, (tm, tm), 1)
        tri_ref[...] = (rr <= cc).astype(BF16)

    logits = _dotx(wr_ref[...], x_ref[...], dims=NT) + br_ref[...]
    eidx = lax.broadcasted_iota(jnp.int32, (ne, tm), 0)
    work = logits
    tops, sels = [], []
    for kk in range(TOP_K):
        mx = jnp.max(work, axis=0, keepdims=True)
        sel_idx = jnp.min(jnp.where(work == mx, eidx, ne), axis=0, keepdims=True)
        sel = eidx == sel_idx
        tops.append(mx)
        sels.append(sel)
        idx_ref[kk:kk + 1, :] = sel_idx
        work = jnp.where(sel, -jnp.inf, work)
    es = [jnp.exp(t - tops[0]) for t in tops]
    den = es[0] + es[1] + es[2] + es[3]
    for kk in range(TOP_K):
        gate_ref[kk:kk + 1, :] = es[kk] / den
    onehot = sels[0] | sels[1] | sels[2] | sels[3]
    oh = jnp.where(onehot, 1.0, 0.0)
    incl = jnp.dot(oh.astype(BF16), tri_ref[...], preferred_element_type=F32)
    before = base_ref[:, 0:1] + incl - oh
    for kk in range(TOP_K):
        rank = jnp.sum(jnp.where(sels[kk], before, 0.0), axis=0, keepdims=True)
        rank_ref[kk:kk + 1, :] = rank.astype(jnp.int32)
    total = base_ref[:, 0:1] + incl[:, tm - 1:tm]
    base_ref[...] = jnp.broadcast_to(total, base_ref.shape)
    cnt_ref[...] = jnp.broadcast_to(total, cnt_ref.shape).astype(jnp.int32)


def moe_route(xf, w_r, b_r, tm=512):
    n, d = xf.shape
    ne = w_r.shape[1]
    tm = min(tm, n)
    slot = lambda dt: jax.ShapeDtypeStruct((TOP_K, n), dt)
    return pl.pallas_call(
        _router_kernel,
        out_shape=(slot(jnp.int32), slot(F32), slot(jnp.int32), jax.ShapeDtypeStruct((ne, LANES), jnp.int32)),
        grid=(n // tm,),
        in_specs=[pl.BlockSpec((tm, d), lambda i: (i, 0)), _const_spec((ne, d)), _const_spec((ne, 1))],
        out_specs=(pl.BlockSpec((TOP_K, tm), lambda i: (0, i)), pl.BlockSpec((TOP_K, tm), lambda i: (0, i)),
                   pl.BlockSpec((TOP_K, tm), lambda i: (0, i)), pl.BlockSpec((ne, LANES), lambda i: (0, 0))),
        scratch_shapes=[pltpu.VMEM((ne, LANES), F32), pltpu.VMEM((tm, tm), BF16)],
        compiler_params=_cparams(("arbitrary",)),
        name="moe_router",
    )(xf, w_r.T, b_r.reshape(ne, 1))


def _dispatch_kernel(dest_ref, npad_ref, pstart_ref, nused_ref, x_ref, zero_ref, xs_ref, sem, zsem):
    i = pl.program_id(0)
    tm = x_ref.shape[0]
    ne = npad_ref.shape[0]
    nblk = xs_ref.shape[0] // MOE_ROWS

    @pl.when(i == 0)
    def _():
        def per_expert(e, carry):
            def one(r, c2):
                pltpu.make_async_copy(zero_ref.at[pl.ds(0, 1)], xs_ref.at[pl.ds(pstart_ref[e] + r, 1)], zsem).start()
                return c2
            lax.fori_loop(0, npad_ref[e], one, 0)

            def one_wait(r, c2):
                pltpu.make_async_copy(zero_ref.at[pl.ds(0, 1)], xs_ref.at[pl.ds(0, 1)], zsem).wait()
                return c2
            lax.fori_loop(0, npad_ref[e], one_wait, 0)
            return carry
        lax.fori_loop(0, ne, per_expert, 0)

        def unused(blk, carry):
            cp = pltpu.make_async_copy(zero_ref, xs_ref.at[pl.ds(blk * MOE_ROWS, MOE_ROWS)], zsem)
            cp.start()
            cp.wait()
            return carry
        lax.fori_loop(nused_ref[0], nblk, unused, 0)

    def issue(t, carry):
        for kk in range(TOP_K):
            pltpu.make_async_copy(x_ref.at[pl.ds(t, 1)], xs_ref.at[pl.ds(dest_ref[kk, t], 1)], sem).start()
        return carry
    lax.fori_loop(0, tm, issue, 0)

    def drain(t, carry):
        for kk in range(TOP_K):
            pltpu.make_async_copy(x_ref.at[pl.ds(0, 1)], xs_ref.at[pl.ds(0, 1)], sem).wait()
        return carry
    lax.fori_loop(0, tm, drain, 0)


def moe_dispatch(xf, dest, npad, pad_fill_start, nused, cap, tm=256):
    n, d = xf.shape
    tm = min(tm, n)
    smem = pl.BlockSpec(memory_space=pltpu.SMEM)
    return pl.pallas_call(
        _dispatch_kernel,
        out_shape=jax.ShapeDtypeStruct((cap, d), F32),
        grid=(n // tm,),
        in_specs=[pl.BlockSpec((TOP_K, tm), lambda i: (0, i), memory_space=pltpu.SMEM), smem, smem, smem,
                  pl.BlockSpec((tm, d), lambda i: (i, 0)), _const_spec((MOE_ROWS, d))],
        out_specs=pl.BlockSpec(memory_space=pl.ANY),
        scratch_shapes=[pltpu.SemaphoreType.DMA, pltpu.SemaphoreType.DMA],
        compiler_params=_cparams(("arbitrary",)),
        name="moe_dispatch",
    )(dest, npad, pad_fill_start, nused, xf, jnp.zeros((MOE_ROWS, d), F32))


def _expert_kernel(be_ref, nused_ref, xs_ref, wg_ref, wu_ref, bg_ref, bu_ref, wd_ref, bd_ref, y_ref):
    i = pl.program_id(0)

    @pl.when(i < nused_ref[0])
    def _():
        x = xs_ref[...].astype(BF16)
        gate = jnp.dot(x, wg_ref[0], preferred_element_type=F32) + bg_ref[0]
        up = jnp.dot(x, wu_ref[0], preferred_element_type=F32) + bu_ref[0]
        gate = jnp.minimum(gate, SWIGLU_LIMIT)
        up = jnp.clip(up, -SWIGLU_LIMIT, SWIGLU_LIMIT)
        act = (up + 1.0) * gate * _sigmoid(SWIGLU_ALPHA * gate)
        y_ref[...] = _bdot(act, wd_ref[0]) + bd_ref[0]

    @pl.when(i >= nused_ref[0])
    def _():
        y_ref[...] = jnp.zeros_like(y_ref)


def moe_experts(xs, block_expert, nused, wg, wu, bg, bu, wd, bd):
    cap, d = xs.shape
    ne, _, dff = wg.shape
    nblk = cap // MOE_ROWS
    blk = lambda i, be, nu: (i, 0)
    ew = lambda shape: pl.BlockSpec((1,) + shape, lambda i, be, nu: (be[i], 0, 0))
    grid_spec = pltpu.PrefetchScalarGridSpec(
        num_scalar_prefetch=2,
        grid=(nblk,),
        in_specs=[pl.BlockSpec((MOE_ROWS, d), blk),
                  ew((d, dff)), ew((d, dff)), ew((1, dff)), ew((1, dff)), ew((dff, d)), ew((1, d))],
        out_specs=pl.BlockSpec((MOE_ROWS, d), blk),
    )
    return pl.pallas_call(
        _expert_kernel,
        out_shape=jax.ShapeDtypeStruct((cap, d), F32),
        grid_spec=grid_spec,
        compiler_params=_cparams(("arbitrary",)),
        name="moe_experts",
    )(block_expert, nused, xs, wg, wu, bg, bu, wd, bd)


def _combine_kernel(dest_ref, y_ref, gate_ref, x_ref, g_ref, b_ref, o_ref, buf_ref, sem):
    tm = x_ref.shape[0]

    def issue(t, carry):
        for kk in range(TOP_K):
            pltpu.make_async_copy(y_ref.at[pl.ds(dest_ref[kk, t], 1)], buf_ref.at[kk, pl.ds(t, 1)], sem).start()
        return carry
    lax.fori_loop(0, tm, issue, 0)

    def drain(t, carry):
        for kk in range(TOP_K):
            pltpu.make_async_copy(y_ref.at[pl.ds(0, 1)], buf_ref.at[kk, pl.ds(0, 1)], sem).wait()
        return carry
    lax.fori_loop(0, tm, drain, 0)

    f = None
    for kk in range(TOP_K):
        t = buf_ref[kk] * gate_ref[:, kk:kk + 1]
        f = t if f is None else f + t
    o_ref[...] = _layer_norm(DEEPNORM_ALPHA * x_ref[...] + f, g_ref[...], b_ref[...])


def moe_combine_ln(y, dest, gate_t, xf, g, b, tm=256):
    n, d = xf.shape
    tm = min(tm, n)
    return pl.pallas_call(
        _combine_kernel,
        out_shape=jax.ShapeDtypeStruct((n, d), F32),
        grid=(n // tm,),
        in_specs=[pl.BlockSpec((TOP_K, tm), lambda i: (0, i), memory_space=pltpu.SMEM),
                  pl.BlockSpec(memory_space=pl.ANY),
                  pl.BlockSpec((tm, TOP_K), lambda i: (i, 0)),
                  pl.BlockSpec((tm, d), lambda i: (i, 0)), _const_spec((1, d)), _const_spec((1, d))],
        out_specs=pl.BlockSpec((tm, d), lambda i: (i, 0)),
        scratch_shapes=[pltpu.VMEM((TOP_K, tm, d), F32), pltpu.SemaphoreType.DMA],
        compiler_params=_cparams(("arbitrary",)),
        name="moe_combine_layernorm",
    )(dest, y, gate_t, xf, g.reshape(1, d), b.reshape(1, d))


def moe_ffn_ln(xf, w_r, b_r, w_gu, b_gu, w_dn, b_dn, g, b):
    n, d = xf.shape
    ne = w_r.shape[1]
    idx, gate, rank, cnt = moe_route(xf, w_r, b_r)
    counts = cnt[:, 0]
    padded = -(-counts // MOE_ROWS) * MOE_ROWS
    pad_end = jnp.cumsum(padded)
    pad_start = pad_end - padded
    cap = n * TOP_K + ne * MOE_ROWS
    nblk = cap // MOE_ROWS
    dest = pad_start[idx] + rank
    nused = (pad_end[-1] // MOE_ROWS).astype(jnp.int32).reshape(1)
    block_expert = jnp.minimum(
        jnp.searchsorted(pad_end, jnp.arange(nblk, dtype=jnp.int32) * MOE_ROWS, side='right'),
        ne - 1).astype(jnp.int32)
    xs = moe_dispatch(xf, dest, (padded - counts).astype(jnp.int32), (pad_start + counts).astype(jnp.int32),
                      nused, cap)
    y = moe_experts(xs, block_expert, nused,
                    w_gu[..., 0::2].astype(BF16), w_gu[..., 1::2].astype(BF16),
                    b_gu[:, None, 0::2], b_gu[:, None, 1::2], w_dn.astype(BF16), b_dn[:, None, :])
    return moe_combine_ln(y, dest, gate.T, xf, g, b)


def kernel(x, mem, ev_w_in, ev_mu, rk_w0, rk_w2, rk_a0, rk_a2, rk_g2, rk_kk, rk_ka, rk_rk, rk_gn_g, rk_gn_b,
           gd_conv, gd_a_log, gd_dt_bias, gd_norm_g, ev_w_out, od_w_in, cv_w, cv_b, cv_ln_g, cv_ln_b, pl_w,
           pl_scale, od_w_out, xa_wq, xa_wk, xa_wv, xa_wo, moe_wr, moe_br, moe_wgu, moe_bgu, moe_wdn, moe_bdn,
           ln_g, ln_b):
    nb, s, d = x.shape
    n = nb * s
    m = mem.shape[1]
    xf = x.reshape(n, d)
    memf = mem.reshape(nb * m, d)
    for layer in range(DEPTH):
        i = layer // 2
        if layer % 2 == 0:
            rwkv_cols = ev_mu.shape[1]
            p3 = matmul(xf, ev_w_in[i].astype(BF16)).reshape(nb, s, -1)
            ya = rwkv_mix(p3, ev_mu[i], rk_w0[i], rk_w2[i], rk_a0[i], rk_a2[i], rk_g2[i], rk_kk[i], rk_ka[i],
                          rk_rk[i].reshape(-1), rk_gn_g[i], rk_gn_b[i])
            yb = gdn_mix(p3, rwkv_cols, gd_conv[i], gd_a_log[i], gd_dt_bias[i], gd_norm_g[i])
            mixed = [ya.reshape(n, -1), yb.reshape(n, -1)]
            w_out = ev_w_out[i]
        else:
            p3 = matmul(xf, od_w_in[i].astype(BF16)).reshape(nb, s, -1)
            mixed = [odd_mix(p3, cv_w[i], cv_b[i], cv_ln_g[i], cv_ln_b[i], pl_w[i], pl_scale[i]).reshape(n, -1)]
            w_out = od_w_out[i]
        xf = mm_res_ln(mixed, w_out.astype(BF16), xf, ln_g[layer, 0], ln_b[layer, 0])
        w_kv = jnp.concatenate([xa_wk[layer], xa_wv[layer]], axis=1).astype(BF16)
        kv3 = matmul(memf, w_kv).reshape(nb, m, 2 * d)
        xf = cross_attention_ln(xf.reshape(nb, s, d), kv3, xa_wq[layer].astype(BF16), xa_wo[layer].astype(BF16),
                                ln_g[layer, 1], ln_b[layer, 1]).reshape(n, d)
        xf = moe_ffn_ln(xf, moe_wr[layer], moe_br[layer], moe_wgu[layer], moe_bgu[layer], moe_wdn[layer],
                        moe_bdn[layer], ln_g[layer, 2], ln_b[layer, 2])
    return xf.reshape(nb, s, d)
```

```python
import functools
import math

import jax
import jax.numpy as jnp
from jax import lax
from jax.experimental import pallas as pl
from jax.experimental.pallas import tpu as pltpu

F32 = jnp.float32
BF16 = jnp.bfloat16

LANES = 128
VMEM_LIMIT = 56 * 1024 * 1024

DEPTH = 2
DEEPNORM_ALPHA = (2 * DEPTH) ** 0.25
LN_EPS = 1e-5
CHUNK = 64
RWKV_HEAD = 64
RWKV_GN_EPS = 64e-5
GDN_HEAD = 128
GDN_CONV = 4
CONV_KERNEL = 31
POOL_WINDOWS = (2, 4, 8, 16)
XA_HEADS = 4
N_EXPERTS = 32
TOP_K = 4
SWIGLU_LIMIT = 7.0
SWIGLU_ALPHA = 1.702
MOE_ROWS_LOG2 = 8
MOE_ROWS = 1 << MOE_ROWS_LOG2

NN = (((1,), (0,)), ((), ()))
NT = (((1,), (1,)), ((), ()))


def _cparams(sem):
    return pltpu.CompilerParams(dimension_semantics=sem, vmem_limit_bytes=VMEM_LIMIT)


def _bdot(a, b, dims=NN):
    return lax.dot_general(a.astype(BF16), b.astype(BF16), dims, preferred_element_type=F32)


def _parts(x, n):
    out, rem = [], x
    for i in range(n):
        h = rem.astype(BF16)
        out.append(h)
        if i + 1 < n:
            rem = rem - h.astype(F32)
    return out


def _dotx(a, b, na=2, nb=2, dims=NN):
    ap, bp = _parts(a, na), _parts(b, nb)
    acc = None
    for i, ai in enumerate(ap):
        for j, bj in enumerate(bp):
            if i + j < max(na, nb):
                t = lax.dot_general(ai, bj, dims, preferred_element_type=F32)
                acc = t if acc is None else acc + t
    return acc


def _layer_norm(v, g, b):
    mu = jnp.mean(v, axis=-1, keepdims=True)
    c = v - mu
    var = jnp.mean(c * c, axis=-1, keepdims=True)
    return c * lax.rsqrt(var + LN_EPS) * g + b


def _sigmoid(x):
    return jax.nn.sigmoid(x)


def _silu(x):
    return x * jax.nn.sigmoid(x)


def _const_spec(shape):
    return pl.BlockSpec(shape, lambda *_: (0,) * len(shape), pipeline_mode=pl.Buffered(1))


def _mm_kernel(x_ref, w_ref, o_ref):
    o_ref[...] = _bdot(x_ref[...], w_ref[...])


def matmul(x, w, tm=1024, tn=512):
    m, k = x.shape
    n = w.shape[1]
    tm = min(tm, m)
    return pl.pallas_call(
        _mm_kernel,
        out_shape=jax.ShapeDtypeStruct((m, n), F32),
        grid=(pl.cdiv(m, tm), pl.cdiv(n, tn)),
        in_specs=[pl.BlockSpec((tm, k), lambda i, j: (i, 0)),
                  pl.BlockSpec((k, tn), lambda i, j: (0, j))],
        out_specs=pl.BlockSpec((tm, tn), lambda i, j: (i, j)),
        compiler_params=_cparams(("parallel", "arbitrary")),
        name="dense_matmul",
    )(x, w)


def _mm_res_ln_kernel(n_in, *refs):
    a_refs = refs[:n_in]
    w_refs = refs[n_in:2 * n_in]
    x_ref, g_ref, b_ref, o_ref = refs[2 * n_in:]
    h = None
    for a_ref, w_ref in zip(a_refs, w_refs):
        t = _bdot(a_ref[...], w_ref[...])
        h = t if h is None else h + t
    o_ref[...] = _layer_norm(DEEPNORM_ALPHA * x_ref[...] + h, g_ref[...], b_ref[...])


def mm_res_ln(a_list, w, x, g, b, tm=256):
    m, d = x.shape
    tm = min(tm, m)
    in_specs, off = [], 0
    for a in a_list:
        in_specs.append(pl.BlockSpec((tm, a.shape[1]), lambda i: (i, 0)))
    w_parts = []
    for a in a_list:
        ka = a.shape[1]
        w_parts.append(lax.slice_in_dim(w, off, off + ka, axis=0))
        in_specs.append(_const_spec((ka, d)))
        off += ka
    in_specs += [pl.BlockSpec((tm, d), lambda i: (i, 0)), _const_spec((1, d)), _const_spec((1, d))]
    return pl.pallas_call(
        functools.partial(_mm_res_ln_kernel, len(a_list)),
        out_shape=jax.ShapeDtypeStruct((m, d), F32),
        grid=(m // tm,),
        in_specs=in_specs,
        out_specs=pl.BlockSpec((tm, d), lambda i: (i, 0)),
        compiler_params=_cparams(("parallel",)),
        name="proj_residual_layernorm",
    )(*a_list, *w_parts, x, g.reshape(1, d), b.reshape(1, d))


MIX_LANES = 2 * LANES


def _group_sum(x, gmat):
    return _dotx(x, gmat, 2, 1)


def _rwkv_kernel(pr_ref, pk_ref, pv_ref, pl_ref, mur_ref, muk_ref, muv_ref, mul_ref,
                 w0_ref, w2_ref, a0_ref, a2_ref, g2_ref, kk_ref, ka_ref, rk_ref, gng_ref, gnb_ref,
                 o_ref, prev_ref, prevl_ref, state_ref):
    c = pl.program_id(1)
    nb, ch, wd = pr_ref.shape

    @pl.when(c == 0)
    def _():
        prev_ref[...] = jnp.zeros_like(prev_ref)
        prevl_ref[...] = jnp.zeros_like(prevl_ref)
        state_ref[...] = jnp.zeros_like(state_ref)

    col = lax.broadcasted_iota(jnp.int32, (ch, LANES), 1)
    mlo = (col < RWKV_HEAD).astype(F32)
    mhi = 1.0 - mlo
    r2 = lax.broadcasted_iota(jnp.int32, (2 * ch, 2 * ch), 0)
    c2 = lax.broadcasted_iota(jnp.int32, (2 * ch, 2 * ch), 1)
    same = (r2 // ch) == (c2 // ch)
    strict = same & ((r2 % ch) > (c2 % ch))
    incl = same & ((r2 % ch) >= (c2 % ch))
    eye = (r2 == c2).astype(F32)
    rg = lax.broadcasted_iota(jnp.int32, (wd, wd), 0)
    cg = lax.broadcasted_iota(jnp.int32, (wd, wd), 1)
    gmat = ((rg // RWKV_HEAD) == (cg // RWKV_HEAD)).astype(BF16)
    rr = lax.broadcasted_iota(jnp.int32, (ch, ch), 0)
    cc = lax.broadcasted_iota(jnp.int32, (ch, ch), 1)
    tri = (rr >= cc).astype(BF16)

    def shifted(x, prev):
        return jnp.where(lax.broadcasted_iota(jnp.int32, x.shape, 0) == 0, prev, pltpu.roll(x, 1, 0))

    def stack(x):
        return jnp.concatenate([x * mlo, x * mhi], axis=0)

    for b in range(nb):
        raw = [pr_ref[b], pk_ref[b], pv_ref[b]]
        rawl = pl_ref[b]
        mus = [mur_ref[...], muk_ref[...], muv_ref[...]]
        mixed = []
        for i in range(3):
            xs = shifted(raw[i], prev_ref[b, i])
            mixed.append(raw[i] + (xs - raw[i]) * mus[i])
        xsl = shifted(rawl, prevl_ref[b])
        lora = rawl + (xsl - rawl) * mul_ref[...]
        for i in range(3):
            prev_ref[b, i] = raw[i][ch - 1:ch]
        prevl_ref[b] = rawl[ch - 1:ch]

        r, k, v = mixed
        l1, l2 = lora[:, :LANES], lora[:, LANES:]
        w = w0_ref[...] + _bdot(jnp.tanh(l1), w2_ref[...])
        ld = -math.exp(-0.5) * _sigmoid(w)
        a = _sigmoid(a0_ref[...] + _bdot(l1, a2_ref[...]))
        g = _bdot(_sigmoid(l2), g2_ref[...])
        kraw = k * kk_ref[...]
        kk = kraw * lax.rsqrt(_group_sum(kraw * kraw, gmat) + 1e-6)
        kmod = k * (1.0 + (a - 1.0) * ka_ref[...])

        cum = _dotx(tri, ld, 1, 3)
        w_t = jnp.exp(cum)
        w_prev = jnp.exp(cum - ld)
        w_inv = jnp.exp(-cum)
        w_end = jnp.exp(cum[ch - 1:ch])
        beta = kk * a
        fa = -kk * w_prev
        fb = beta * w_inv
        fk = kmod * w_inv
        fq = r * w_t
        fbh = fb * w_end
        fkh = fk * w_end

        ys = []
        for hp in range(wd // LANES):
            sl = slice(hp * LANES, (hp + 1) * LANES)
            a2, b2, k2, q2, v2 = stack(fa[:, sl]), stack(fb[:, sl]), stack(fk[:, sl]), stack(fq[:, sl]), stack(v[:, sl])
            bh2, kh2 = stack(fbh[:, sl]), stack(fkh[:, sl])
            lab = jnp.where(strict, _bdot(a2, b2, dims=NT), 0.0)
            lak = jnp.where(strict, _bdot(a2, k2, dims=NT), 0.0)
            grb = jnp.where(incl, _bdot(q2, b2, dims=NT), 0.0)
            grk = jnp.where(incl, _bdot(q2, k2, dims=NT), 0.0)
            minv = eye + lab
            pw = lab
            for _ in range(5):
                pw = _bdot(pw, pw)
                minv = minv + _bdot(minv, pw)
            t0 = state_ref[b, hp]
            u2 = _bdot(minv, _bdot(a2, t0) + _bdot(lak, v2))
            y2 = _bdot(q2, t0) + _bdot(grb, u2) + _bdot(grk, v2)
            state_ref[b, hp] = w_end[:, sl].T * t0 + _bdot(bh2.T, u2) + _bdot(kh2.T, v2)
            ys.append(y2[:ch] + y2[ch:])
        y = jnp.concatenate(ys, axis=1)

        inv_n = 1.0 / RWKV_HEAD
        mean = _group_sum(y, gmat) * inv_n
        yc = y - mean
        var = _group_sum(yc * yc, gmat) * inv_n
        yn = yc * lax.rsqrt(var + RWKV_GN_EPS) * gng_ref[...] + gnb_ref[...]
        bonus = _group_sum(r * kmod * rk_ref[...], gmat) * v
        o_ref[b] = (yn + bonus) * g


def rwkv_mix(p3, mu, w0, w2, a0, a2, g2, k_k, k_a, r_k, gn_g, gn_b):
    nb, s, _ = p3.shape
    width = w0.shape[0]
    wd = MIX_LANES
    nstep = width // wd
    rank_w, rank_a = w2.shape[0], a2.shape[0]
    assert rank_w + rank_a == LANES and g2.shape[0] == LANES and 2 * CHUNK == LANES and 2 * RWKV_HEAD == LANES
    assert width % wd == 0 and (3 * width) % (2 * LANES) == 0
    w2p = jnp.concatenate([w2, jnp.zeros((rank_a, width), F32)], 0)
    a2p = jnp.concatenate([jnp.zeros((rank_w, width), F32), a2], 0)
    row = lambda t: t.reshape(1, -1)
    blk = lambda off: pl.BlockSpec((nb, CHUNK, wd), lambda j, c: (0, c, off + j))
    vec = lambda off: pl.BlockSpec((1, wd), lambda j, c: (0, off + j))
    mat = pl.BlockSpec((LANES, wd), lambda j, c: (0, j))
    lora_blk = 3 * width // (2 * LANES)
    mu2 = row(mu)
    return pl.pallas_call(
        _rwkv_kernel,
        out_shape=jax.ShapeDtypeStruct((nb, s, width), F32),
        grid=(nstep, s // CHUNK),
        in_specs=[blk(0), blk(nstep), blk(2 * nstep),
                  pl.BlockSpec((nb, CHUNK, 2 * LANES), lambda j, c: (0, c, lora_blk)),
                  vec(0), vec(nstep), vec(2 * nstep),
                  pl.BlockSpec((1, 2 * LANES), lambda j, c: (0, lora_blk)),
                  vec(0), mat, vec(0), mat, mat, vec(0), vec(0), vec(0), vec(0), vec(0)],
        out_specs=pl.BlockSpec((nb, CHUNK, wd), lambda j, c: (0, c, j)),
        scratch_shapes=[pltpu.VMEM((nb, 3, 1, wd), F32),
                        pltpu.VMEM((nb, 1, 2 * LANES), F32),
                        pltpu.VMEM((nb, wd // LANES, LANES, LANES), F32)],
        compiler_params=_cparams(("parallel", "arbitrary")),
        name="rwkv7_mixer",
    )(p3, p3, p3, p3, mu2, mu2, mu2, mu2, row(w0), w2p, row(a0), a2p, g2,
      row(k_k), row(k_a), row(r_k), row(gn_g), row(gn_b))


def _gdn_kernel(pq_ref, pk_ref, pv_ref, pg_ref, pbd_ref, cq_ref, ck_ref, cv_ref, alog_ref, dtb_ref,
                ng_ref, o_ref, halo_ref, state_ref):
    j = pl.program_id(0)
    c = pl.program_id(1)
    nb, ch, wd = pq_ref.shape
    nheads = alog_ref.shape[1]
    hps = wd // GDN_HEAD

    @pl.when(c == 0)
    def _():
        halo_ref[...] = jnp.zeros_like(halo_ref)
        state_ref[...] = jnp.zeros_like(state_ref)

    col = lax.broadcasted_iota(jnp.int32, (ch, LANES), 1)
    rr = lax.broadcasted_iota(jnp.int32, (ch, ch), 0)
    cc = lax.broadcasted_iota(jnp.int32, (ch, ch), 1)
    causal = rr >= cc
    strict = rr > cc
    tri = causal.astype(BF16)
    triu = (rr <= cc).astype(BF16)
    eye = (rr == cc).astype(F32)
    ones = jnp.ones((LANES, LANES), BF16)
    hcol = lax.broadcasted_iota(jnp.int32, (1, nheads), 1)

    def conv_silu(x_ref, w_ref, b, i):
        halo_ref[b, i, 8:, :] = x_ref[b]
        acc = None
        for t in range(GDN_CONV):
            lo = 8 - (GDN_CONV - 1) + t
            term = halo_ref[b, i, lo:lo + ch, :] * w_ref[t:t + 1, :]
            acc = term if acc is None else acc + term
        halo_ref[b, i, 0:8, :] = halo_ref[b, i, ch:ch + 8, :]
        return _silu(acc)

    for b in range(nb):
        qf = conv_silu(pq_ref, cq_ref, b, 0)
        kf = conv_silu(pk_ref, ck_ref, b, 1)
        vf = conv_silu(pv_ref, cv_ref, b, 2)
        bd = pbd_ref[b]
        for hh in range(hps):
            h = j * hps + hh
            sl = slice(hh * GDN_HEAD, (hh + 1) * GDN_HEAD)
            q, k, v = qf[:, sl], kf[:, sl], vf[:, sl]
            a_coef = -jnp.exp(jnp.sum(jnp.where(hcol == h, alog_ref[...], 0.0), axis=-1, keepdims=True))
            dt_b = jnp.sum(jnp.where(hcol == h, dtb_ref[...], 0.0), axis=-1, keepdims=True)
            q = q * lax.rsqrt(_dotx(q * q, ones, 2, 1) + 1e-6) * (GDN_HEAD ** -0.5)
            k = k * lax.rsqrt(_dotx(k * k, ones, 2, 1) + 1e-6)
            bcol = jnp.sum(jnp.where(col == h, bd, 0.0), axis=-1, keepdims=True)
            dcol = jnp.sum(jnp.where(col == h + nheads, bd, 0.0), axis=-1, keepdims=True)
            beta = _sigmoid(bcol)
            z = dcol + dt_b
            softplus = jnp.maximum(z, 0.0) + jnp.log(1.0 + jnp.exp(-jnp.abs(z)))
            la = jnp.broadcast_to(a_coef * softplus, (ch, LANES))
            gc = _dotx(tri, la, 1, 3)
            la_rows = jnp.broadcast_to(la.T[0:1, :], (ch, ch))
            gc_row = _dotx(la_rows, triu, 3, 1)
            diff = gc[:, :ch] - gc_row
            decay = jnp.where(causal, jnp.exp(jnp.where(causal, diff, 0.0)), 0.0)
            eg = jnp.exp(gc)
            g_last = eg[ch - 1:ch, :]
            kb = k * beta
            lower = jnp.where(strict, _bdot(kb, k, dims=NT) * decay, 0.0)
            tinv = eye - lower
            pw = -lower
            for _ in range(5):
                pw = _bdot(pw, pw)
                tinv = tinv + _bdot(tinv, pw)
            u = _bdot(tinv, v * beta)
            w = _bdot(tinv, kb * eg)
            qk = _bdot(q, k, dims=NT) * decay
            q_dec = q * eg
            k_dec = k * jnp.exp(gc[ch - 1:ch, :] - gc)
            t0 = state_ref[b, hh]
            v_new = u - _bdot(w, t0)
            out = _bdot(q_dec, t0) + _bdot(qk, v_new)
            state_ref[b, hh] = t0 * g_last[:, 0:1] + _bdot(k_dec.T, v_new)
            ms = _dotx(out * out, ones, 2, 1) * (1.0 / GDN_HEAD)
            o = out * lax.rsqrt(ms + 1e-6) * ng_ref[...]
            o_ref[b, :, sl] = o * _silu(pg_ref[b, :, sl])


def gdn_mix(p3, col0, conv_w, a_log, dt_bias, norm_g):
    nb, s, _ = p3.shape
    nheads = a_log.shape[0]
    width = nheads * GDN_HEAD
    wd = MIX_LANES
    nstep = width // wd
    assert col0 % wd == 0 and width % wd == 0 and GDN_HEAD == LANES
    o0 = col0 // wd
    blk = lambda off: pl.BlockSpec((nb, CHUNK, wd), lambda j, c: (0, c, off + j))
    cw = lambda off: pl.BlockSpec((GDN_CONV, wd), lambda j, c: (0, off + j))
    return pl.pallas_call(
        _gdn_kernel,
        out_shape=jax.ShapeDtypeStruct((nb, s, width), F32),
        grid=(nstep, s // CHUNK),
        in_specs=[blk(o0), blk(o0 + nstep), blk(o0 + 2 * nstep), blk(o0 + 3 * nstep),
                  pl.BlockSpec((nb, CHUNK, LANES), lambda j, c: (0, c, (col0 + 4 * width) // LANES)),
                  cw(0), cw(nstep), cw(2 * nstep),
                  pl.BlockSpec((1, nheads), lambda j, c: (0, 0)),
                  pl.BlockSpec((1, nheads), lambda j, c: (0, 0)),
                  pl.BlockSpec((1, LANES), lambda j, c: (0, 0))],
        out_specs=pl.BlockSpec((nb, CHUNK, wd), lambda j, c: (0, c, j)),
        scratch_shapes=[pltpu.VMEM((nb, 3, 8 + CHUNK, wd), F32),
                        pltpu.VMEM((nb, wd // GDN_HEAD, LANES, LANES), F32)],
        compiler_params=_cparams(("parallel", "arbitrary")),
        name="gated_deltanet_mixer",
    )(p3, p3, p3, p3, p3, conv_w, conv_w, conv_w, a_log.reshape(1, -1), dt_bias.reshape(1, -1),
      norm_g.reshape(1, -1))


HALO_U = 32
HALO_Z = 16


def _odd_kernel(p_ref, cvw_ref, cvb_ref, lng_ref, lnb_ref, plw_ref, pls_ref, o_ref, ubuf_ref, zbuf_ref):
    s_idx = pl.program_id(1)
    ts = p_ref.shape[1]
    cw = cvb_ref.shape[1]
    pool_w = pls_ref.shape[1]
    pg = pool_w // len(POOL_WINDOWS)

    @pl.when(s_idx == 0)
    def _():
        ubuf_ref[0:HALO_U, :] = jnp.zeros((HALO_U, cw), F32)
        zbuf_ref[0:HALO_Z, :] = jnp.zeros((HALO_Z, pool_w), F32)

    pa = p_ref[0, :, 0:cw]
    pb = p_ref[0, :, cw:2 * cw]
    ubuf_ref[HALO_U:, :] = pa * _sigmoid(pb)
    acc = jnp.broadcast_to(cvb_ref[...], (ts, cw))
    base = HALO_U - (CONV_KERNEL - 1)
    for j in range(CONV_KERNEL):
        acc = acc + ubuf_ref[base + j:base + j + ts, :] * cvw_ref[j:j + 1, :]
    ubuf_ref[0:HALO_U, :] = ubuf_ref[ts:ts + HALO_U, :]
    o_ref[0, :, 0:cw] = _silu(_layer_norm(acc, lng_ref[...], lnb_ref[...]))

    z = p_ref[0, :, 2 * cw:]
    zbuf_ref[HALO_Z:, :] = z
    t1 = (s_idx * ts + 1 + lax.broadcasted_iota(jnp.int32, (ts, 1), 0)).astype(F32)
    for gi, win in enumerate(POOL_WINDOWS):
        lo = gi * pg
        ssum = None
        for j in range(win):
            t = zbuf_ref[HALO_Z - j:HALO_Z - j + ts, lo:lo + pg]
            ssum = t if ssum is None else ssum + t
        pooled = ssum / jnp.minimum(t1, float(win)) - z[:, lo:lo + pg]
        mixed = _bdot(pooled, plw_ref[gi]) * pls_ref[:, lo:lo + pg]
        o_ref[0, :, cw + lo:cw + lo + pg] = mixed
    zbuf_ref[0:HALO_Z, :] = zbuf_ref[ts:ts + HALO_Z, :]


def odd_mix(p3, cv_w, cv_b, cv_ln_g, cv_ln_b, pl_w, pl_scale, ts=256):
    nb, s, cols = p3.shape
    cw = cv_b.shape[0]
    pool_w = pl_scale.shape[0]
    ts = min(ts, s)
    row = lambda t: t.reshape(1, -1)
    cvw = jnp.concatenate([cv_w, jnp.zeros((HALO_U - CONV_KERNEL, cw), F32)], 0)
    return pl.pallas_call(
        _odd_kernel,
        out_shape=jax.ShapeDtypeStruct((nb, s, cw + pool_w), F32),
        grid=(nb, s // ts),
        in_specs=[pl.BlockSpec((1, ts, cols), lambda b, i: (b, i, 0)),
                  _const_spec((HALO_U, cw)), _const_spec((1, cw)), _const_spec((1, cw)), _const_spec((1, cw)),
                  _const_spec(pl_w.shape), _const_spec((1, pool_w))],
        out_specs=pl.BlockSpec((1, ts, cw + pool_w), lambda b, i: (b, i, 0)),
        scratch_shapes=[pltpu.VMEM((HALO_U + ts, cw), F32), pltpu.VMEM((HALO_Z + ts, pool_w), F32)],
        compiler_params=_cparams(("parallel", "arbitrary")),
        name="conformer_pool_mixer",
    )(p3, cvw, row(cv_b), row(cv_ln_g), row(cv_ln_b), pl_w.astype(BF16), row(pl_scale))


def _xattn_kernel(x_ref, k_ref, v_ref, wq_ref, wo_ref, g_ref, b_ref, o_ref):
    x = x_ref[0]
    d = x.shape[1]
    dh = d // XA_HEADS
    q = _bdot(x, wq_ref[...])
    heads = []
    for hd in range(XA_HEADS):
        sl = slice(hd * dh, (hd + 1) * dh)
        s = _bdot(q[:, sl], k_ref[0, :, sl], NT) * (dh ** -0.5)
        s = s - jnp.max(s, axis=-1, keepdims=True)
        e = jnp.exp(s)
        prob = e / jnp.sum(e, axis=-1, keepdims=True)
        heads.append(_bdot(prob, v_ref[0, :, sl]))
    a = _bdot(jnp.concatenate(heads, axis=-1), wo_ref[...])
    o_ref[0] = _layer_norm(DEEPNORM_ALPHA * x + a, g_ref[...], b_ref[...])


def cross_attention_ln(x3, k3, v3, wq, wo, g, b, ts=256):
    nb, s, d = x3.shape
    m = k3.shape[1]
    ts = min(ts, s)
    return pl.pallas_call(
        _xattn_kernel,
        out_shape=jax.ShapeDtypeStruct((nb, s, d), F32),
        grid=(nb, s // ts),
        in_specs=[pl.BlockSpec((1, ts, d), lambda bi, i: (bi, i, 0)),
                  pl.BlockSpec((1, m, d), lambda bi, i: (bi, 0, 0)),
                  pl.BlockSpec((1, m, d), lambda bi, i: (bi, 0, 0)),
                  _const_spec((d, d)), _const_spec((d, d)), _const_spec((1, d)), _const_spec((1, d))],
        out_specs=pl.BlockSpec((1, ts, d), lambda bi, i: (bi, i, 0)),
        compiler_params=_cparams(("parallel", "parallel")),
        name="cross_attention_layernorm",
    )(x3, k3, v3, wq, wo, g.reshape(1, d), b.reshape(1, d))


def _router_kernel(x_ref, wr_ref, br_ref, idx_ref, gate_ref, rank_ref, cnt_ref, base_ref, tri_ref):
    i = pl.program_id(0)
    tm = x_ref.shape[0]
    ne = wr_ref.shape[0]

    @pl.when(i == 0)
    def _():
        base_ref[...] = jnp.zeros_like(base_ref)
        rr = lax.broadcasted_iota(jnp.int32, (tm, tm), 0)
        cc = lax.broadcasted_iota(jnp.int32, (tm, tm), 1)
        tri_ref[...] = (rr <= cc).astype(BF16)

    logits = _dotx(wr_ref[...], x_ref[...], dims=NT) + br_ref[...]
    eidx = lax.broadcasted_iota(jnp.int32, (ne, tm), 0)
    work = logits
    tops, sels = [], []
    for kk in range(TOP_K):
        mx = jnp.max(work, axis=0, keepdims=True)
        sel_idx = jnp.min(jnp.where(work == mx, eidx, ne), axis=0, keepdims=True)
        sel = eidx == sel_idx
        tops.append(mx)
        sels.append(sel)
        idx_ref[kk:kk + 1, :] = sel_idx
        work = jnp.where(sel, -jnp.inf, work)
    es = [jnp.exp(t - tops[0]) for t in tops]
    den = es[0] + es[1] + es[2] + es[3]
    for kk in range(TOP_K):
        gate_ref[kk:kk + 1, :] = es[kk] / den
    onehot = sels[0] | sels[1] | sels[2] | sels[3]
    oh = jnp.where(onehot, 1.0, 0.0)
    incl = jnp.dot(oh.astype(BF16), tri_ref[...], preferred_element_type=F32)
    before = base_ref[:, 0:1] + incl - oh
    for kk in range(TOP_K):
        rank = jnp.sum(jnp.where(sels[kk], before, 0.0), axis=0, keepdims=True)
        rank_ref[kk:kk + 1, :] = rank.astype(jnp.int32)
    total = base_ref[:, 0:1] + incl[:, tm - 1:tm]
    base_ref[...] = jnp.broadcast_to(total, base_ref.shape)
    cnt_ref[...] = jnp.broadcast_to(total, cnt_ref.shape).astype(jnp.int32)


def moe_route(xf, w_r, b_r, tm=512):
    n, d = xf.shape
    ne = w_r.shape[1]
    tm = min(tm, n)
    slot = lambda dt: jax.ShapeDtypeStruct((TOP_K, n), dt)
    return pl.pallas_call(
        _router_kernel,
        out_shape=(slot(jnp.int32), slot(F32), slot(jnp.int32), jax.ShapeDtypeStruct((ne, LANES), jnp.int32)),
        grid=(n // tm,),
        in_specs=[pl.BlockSpec((tm, d), lambda i: (i, 0)), _const_spec((ne, d)), _const_spec((ne, 1))],
        out_specs=(pl.BlockSpec((TOP_K, tm), lambda i: (0, i)), pl.BlockSpec((TOP_K, tm), lambda i: (0, i)),
                   pl.BlockSpec((TOP_K, tm), lambda i: (0, i)), pl.BlockSpec((ne, LANES), lambda i: (0, 0))),
        scratch_shapes=[pltpu.VMEM((ne, LANES), F32), pltpu.VMEM((tm, tm), BF16)],
        compiler_params=_cparams(("arbitrary",)),
        name="moe_router",
    )(xf, w_r.T, b_r.reshape(ne, 1))


def _dispatch_kernel(dest_ref, npad_ref, pstart_ref, nused_ref, x_ref, zero_ref, xs_ref, sem, zsem):
    i = pl.program_id(0)
    tm = x_ref.shape[0]
    ne = npad_ref.shape[0]
    nblk = xs_ref.shape[0] // MOE_ROWS

    @pl.when(i == 0)
    def _():
        def per_expert(e, carry):
            def one(r, c2):
                pltpu.make_async_copy(zero_ref.at[pl.ds(0, 1)], xs_ref.at[pl.ds(pstart_ref[e] + r, 1)], zsem).start()
                return c2
            lax.fori_loop(0, npad_ref[e], one, 0)

            def one_wait(r, c2):
                pltpu.make_async_copy(zero_ref.at[pl.ds(0, 1)], xs_ref.at[pl.ds(0, 1)], zsem).wait()
                return c2
            lax.fori_loop(0, npad_ref[e], one_wait, 0)
            return carry
        lax.fori_loop(0, ne, per_expert, 0)

        def unused(blk, carry):
            cp = pltpu.make_async_copy(zero_ref, xs_ref.at[pl.ds(blk * MOE_ROWS, MOE_ROWS)], zsem)
            cp.start()
            cp.wait()
            return carry
        lax.fori_loop(nused_ref[0], nblk, unused, 0)

    def issue(t, carry):
        for kk in range(TOP_K):
            pltpu.make_async_copy(x_ref.at[pl.ds(t, 1)], xs_ref.at[pl.ds(dest_ref[kk, t], 1)], sem).start()
        return carry
    lax.fori_loop(0, tm, issue, 0)

    def drain(t, carry):
        for kk in range(TOP_K):
            pltpu.make_async_copy(x_ref.at[pl.ds(0, 1)], xs_ref.at[pl.ds(0, 1)], sem).wait()
        return carry
    lax.fori_loop(0, tm, drain, 0)


def moe_dispatch(xf, dest, npad, pad_fill_start, nused, cap, tm=256):
    n, d = xf.shape
    tm = min(tm, n)
    smem = pl.BlockSpec(memory_space=pltpu.SMEM)
    return pl.pallas_call(
        _dispatch_kernel,
        out_shape=jax.ShapeDtypeStruct((cap, d), F32),
        grid=(n // tm,),
        in_specs=[pl.BlockSpec((TOP_K, tm), lambda i: (0, i), memory_space=pltpu.SMEM), smem, smem, smem,
                  pl.BlockSpec((tm, d), lambda i: (i, 0)), _const_spec((MOE_ROWS, d))],
        out_specs=pl.BlockSpec(memory_space=pl.ANY),
        scratch_shapes=[pltpu.SemaphoreType.DMA, pltpu.SemaphoreType.DMA],
        compiler_params=_cparams(("arbitrary",)),
        name="moe_dispatch",
    )(dest, npad, pad_fill_start, nused, xf, jnp.zeros((MOE_ROWS, d), F32))


PAIR_TILE = 2 * LANES


def _expert_changed(be_ref, i):
    return (i == 0) | (be_ref[i] != be_ref[jnp.maximum(i - 1, 0)])


def _expert_gu_kernel(be_ref, nused_ref, xs_ref, wgu_ref, bgu_ref, act_ref, wperm_ref, bperm_ref):
    i = pl.program_id(0)
    active = i < nused_ref[0]
    ntile = wgu_ref.shape[2] // PAIR_TILE

    @pl.when(active & _expert_changed(be_ref, i))
    def _():
        r = lax.broadcasted_iota(jnp.int32, (PAIR_TILE, PAIR_TILE), 0)
        c = lax.broadcasted_iota(jnp.int32, (PAIR_TILE, PAIR_TILE), 1)
        perm = (r == jnp.where(c < LANES, 2 * c, 2 * (c - LANES) + 1)).astype(BF16)
        for t in range(ntile):
            sl = slice(t * PAIR_TILE, (t + 1) * PAIR_TILE)
            w = wgu_ref[0, :, sl].astype(BF16)
            wperm_ref[:, sl] = jnp.dot(w, perm, preferred_element_type=F32).astype(BF16)
            bias = jnp.broadcast_to(bgu_ref[0, :, sl], (8, PAIR_TILE))
            bperm_ref[:, sl] = _dotx(bias, perm, 3, 1)

    @pl.when(active)
    def _():
        x = xs_ref[...].astype(BF16)
        for t in range(ntile):
            sl = slice(t * PAIR_TILE, (t + 1) * PAIR_TILE)
            h = jnp.dot(x, wperm_ref[:, sl], preferred_element_type=F32) + bperm_ref[0:1, sl]
            gate = jnp.minimum(h[:, :LANES], SWIGLU_LIMIT)
            up = jnp.clip(h[:, LANES:], -SWIGLU_LIMIT, SWIGLU_LIMIT)
            act = (up + 1.0) * gate * _sigmoid(SWIGLU_ALPHA * gate)
            act_ref[:, t * LANES:(t + 1) * LANES] = act.astype(BF16)

    @pl.when(jnp.logical_not(active))
    def _():
        act_ref[...] = jnp.zeros_like(act_ref)


def _expert_dn_kernel(be_ref, nused_ref, act_ref, wdn_ref, bdn_ref, y_ref, wd_ref):
    i = pl.program_id(0)
    active = i < nused_ref[0]

    @pl.when(active & _expert_changed(be_ref, i))
    def _():
        wd_ref[...] = wdn_ref[0].astype(BF16)

    @pl.when(active)
    def _():
        y_ref[...] = jnp.dot(act_ref[...], wd_ref[...], preferred_element_type=F32) + bdn_ref[0]

    @pl.when(jnp.logical_not(active))
    def _():
        y_ref[...] = jnp.zeros_like(y_ref)


def moe_experts(xs, block_expert, nused, w_gu, b_gu, w_dn, b_dn):
    cap, d = xs.shape
    ne, _, d2 = w_gu.shape
    dff = d2 // 2
    nblk = cap // MOE_ROWS
    blk = lambda i, be, nu: (i, 0)
    ew = lambda shape: pl.BlockSpec((1,) + shape, lambda i, be, nu: (be[i], 0, 0))
    act = pl.pallas_call(
        _expert_gu_kernel,
        out_shape=jax.ShapeDtypeStruct((cap, dff), BF16),
        grid_spec=pltpu.PrefetchScalarGridSpec(
            num_scalar_prefetch=2, grid=(nblk,),
            in_specs=[pl.BlockSpec((MOE_ROWS, d), blk), ew((d, d2)), ew((1, d2))],
            out_specs=pl.BlockSpec((MOE_ROWS, dff), blk),
            scratch_shapes=[pltpu.VMEM((d, d2), BF16), pltpu.VMEM((8, d2), F32)]),
        compiler_params=_cparams(("arbitrary",)),
        name="moe_expert_gate_up",
    )(block_expert, nused, xs, w_gu, b_gu.reshape(ne, 1, d2))
    return pl.pallas_call(
        _expert_dn_kernel,
        out_shape=jax.ShapeDtypeStruct((cap, d), F32),
        grid_spec=pltpu.PrefetchScalarGridSpec(
            num_scalar_prefetch=2, grid=(nblk,),
            in_specs=[pl.BlockSpec((MOE_ROWS, dff), blk), ew((dff, d)), ew((1, d))],
            out_specs=pl.BlockSpec((MOE_ROWS, d), blk),
            scratch_shapes=[pltpu.VMEM((dff, d), BF16)]),
        compiler_params=_cparams(("arbitrary",)),
        name="moe_expert_down",
    )(block_expert, nused, act, w_dn, b_dn.reshape(ne, 1, d))


def _plan_kernel(idx_ref, rank_ref, cnt_ref, dest_ref):
    ne = cnt_ref.shape[0]
    tm = idx_ref.shape[1]
    padded = ((cnt_ref[...] + (MOE_ROWS - 1)) >> MOE_ROWS_LOG2) << MOE_ROWS_LOG2
    r = lax.broadcasted_iota(jnp.int32, (ne, ne), 0)
    c = lax.broadcasted_iota(jnp.int32, (ne, ne), 1)
    pad_start = _dotx((c < r).astype(BF16), padded.astype(F32), 1, 3)[:, 0:1]
    eidx = lax.broadcasted_iota(jnp.int32, (ne, tm), 0)
    for kk in range(TOP_K):
        start = jnp.sum(jnp.where(eidx == idx_ref[kk:kk + 1, :], pad_start, 0.0), axis=0, keepdims=True)
        dest_ref[kk:kk + 1, :] = rank_ref[kk:kk + 1, :] + start.astype(jnp.int32)


def moe_plan(idx, rank, cnt, tm=2048):
    k, n = idx.shape
    ne = cnt.shape[0]
    tm = min(tm, n)
    spec = pl.BlockSpec((k, tm), lambda i: (0, i))
    return pl.pallas_call(
        _plan_kernel,
        out_shape=jax.ShapeDtypeStruct((k, n), jnp.int32),
        grid=(n // tm,),
        in_specs=[spec, spec, _const_spec((ne, LANES))],
        out_specs=spec,
        compiler_params=_cparams(("parallel",)),
        name="moe_plan",
    )(idx, rank, cnt)


def _combine_kernel(dest_ref, y_ref, gate_ref, x_ref, g_ref, b_ref, o_ref, buf_ref, sem):
    tm = x_ref.shape[0]

    def issue(t, carry):
        for kk in range(TOP_K):
            pltpu.make_async_copy(y_ref.at[pl.ds(dest_ref[kk, t], 1)], buf_ref.at[kk, pl.ds(t, 1)], sem).start()
        return carry
    lax.fori_loop(0, tm, issue, 0)

    def drain(t, carry):
        for kk in range(TOP_K):
            pltpu.make_async_copy(y_ref.at[pl.ds(0, 1)], buf_ref.at[kk, pl.ds(0, 1)], sem).wait()
        return carry
    lax.fori_loop(0, tm, drain, 0)

    f = None
    for kk in range(TOP_K):
        t = buf_ref[kk] * gate_ref[:, kk:kk + 1]
        f = t if f is None else f + t
    o_ref[...] = _layer_norm(DEEPNORM_ALPHA * x_ref[...] + f, g_ref[...], b_ref[...])


def moe_combine_ln(y, dest, gate_t, xf, g, b, tm=256):
    n, d = xf.shape
    tm = min(tm, n)
    return pl.pallas_call(
        _combine_kernel,
        out_shape=jax.ShapeDtypeStruct((n, d), F32),
        grid=(n // tm,),
        in_specs=[pl.BlockSpec((TOP_K, tm), lambda i: (0, i), memory_space=pltpu.SMEM),
                  pl.BlockSpec(memory_space=pl.ANY),
                  pl.BlockSpec((tm, TOP_K), lambda i: (i, 0)),
                  pl.BlockSpec((tm, d), lambda i: (i, 0)), _const_spec((1, d)), _const_spec((1, d))],
        out_specs=pl.BlockSpec((tm, d), lambda i: (i, 0)),
        scratch_shapes=[pltpu.VMEM((TOP_K, tm, d), F32), pltpu.SemaphoreType.DMA],
        compiler_params=_cparams(("arbitrary",)),
        name="moe_combine_layernorm",
    )(dest, y, gate_t, xf, g.reshape(1, d), b.reshape(1, d))


def moe_ffn_ln(xf, w_r, b_r, w_gu, b_gu, w_dn, b_dn, g, b):
    n, d = xf.shape
    ne = w_r.shape[1]
    idx, gate, rank, cnt = moe_route(xf, w_r, b_r)
    dest = moe_plan(idx, rank, cnt)
    counts = cnt[:, 0]
    padded = -(-counts // MOE_ROWS) * MOE_ROWS
    pad_end = jnp.cumsum(padded)
    pad_start = pad_end - padded
    cap = n * TOP_K + ne * MOE_ROWS
    nblk = cap // MOE_ROWS
    nused = (pad_end[-1] // MOE_ROWS).astype(jnp.int32).reshape(1)
    blk_row = jnp.arange(nblk, dtype=jnp.int32) * MOE_ROWS
    block_expert = jnp.minimum(jnp.sum(blk_row[:, None] >= pad_end[None, :], axis=1), ne - 1).astype(jnp.int32)
    xs = moe_dispatch(xf, dest, (padded - counts).astype(jnp.int32), (pad_start + counts).astype(jnp.int32),
                      nused, cap)
    y = moe_experts(xs, block_expert, nused, w_gu, b_gu, w_dn, b_dn)
    return moe_combine_ln(y, dest, gate.T, xf, g, b)


def kernel(x, mem, ev_w_in, ev_mu, rk_w0, rk_w2, rk_a0, rk_a2, rk_g2, rk_kk, rk_ka, rk_rk, rk_gn_g, rk_gn_b,
           gd_conv, gd_a_log, gd_dt_bias, gd_norm_g, ev_w_out, od_w_in, cv_w, cv_b, cv_ln_g, cv_ln_b, pl_w,
           pl_scale, od_w_out, xa_wq, xa_wk, xa_wv, xa_wo, moe_wr, moe_br, moe_wgu, moe_bgu, moe_wdn, moe_bdn,
           ln_g, ln_b):
    nb, s, d = x.shape
    n = nb * s
    m = mem.shape[1]
    xf = x.reshape(n, d)
    memf = mem.reshape(nb * m, d)
    for layer in range(DEPTH):
        i = layer // 2
        if layer % 2 == 0:
            rwkv_cols = ev_mu.shape[1]
            p3 = matmul(xf, ev_w_in[i]).reshape(nb, s, -1)
            ya = rwkv_mix(p3, ev_mu[i], rk_w0[i], rk_w2[i], rk_a0[i], rk_a2[i], rk_g2[i], rk_kk[i], rk_ka[i],
                          rk_rk[i].reshape(-1), rk_gn_g[i], rk_gn_b[i])
            yb = gdn_mix(p3, rwkv_cols, gd_conv[i], gd_a_log[i], gd_dt_bias[i], gd_norm_g[i])
            mixed = [ya.reshape(n, -1), yb.reshape(n, -1)]
            w_out = ev_w_out[i]
        else:
            p3 = matmul(xf, od_w_in[i]).reshape(nb, s, -1)
            mixed = [odd_mix(p3, cv_w[i], cv_b[i], cv_ln_g[i], cv_ln_b[i], pl_w[i], pl_scale[i]).reshape(n, -1)]
            w_out = od_w_out[i]
        xf = mm_res_ln(mixed, w_out.astype(BF16), xf, ln_g[layer, 0], ln_b[layer, 0])
        k3 = matmul(memf, xa_wk[layer]).reshape(nb, m, d)
        v3 = matmul(memf, xa_wv[layer]).reshape(nb, m, d)
        xf = cross_attention_ln(xf.reshape(nb, s, d), k3, v3, xa_wq[layer].astype(BF16), xa_wo[layer].astype(BF16),
                                ln_g[layer, 1], ln_b[layer, 1]).reshape(n, d)
        xf = moe_ffn_ln(xf, moe_wr[layer], moe_br[layer], moe_wgu[layer], moe_bgu[layer], moe_wdn[layer],
                        moe_bdn[layer], ln_g[layer, 2], ln_b[layer, 2])
    return xf.reshape(nb, s, d)
```

```python
import functools
import math

import jax
import jax.numpy as jnp
from jax import lax
from jax.experimental import pallas as pl
from jax.experimental.pallas import tpu as pltpu

F32 = jnp.float32
BF16 = jnp.bfloat16

LANES = 128
VMEM_LIMIT = 56 * 1024 * 1024

DEPTH = 2
DEEPNORM_ALPHA = (2 * DEPTH) ** 0.25
LN_EPS = 1e-5
CHUNK = 64
RWKV_HEAD = 64
RWKV_GN_EPS = 64e-5
GDN_HEAD = 128
GDN_CONV = 4
CONV_KERNEL = 31
POOL_WINDOWS = (2, 4, 8, 16)
XA_HEADS = 4
N_EXPERTS = 32
TOP_K = 4
SWIGLU_LIMIT = 7.0
SWIGLU_ALPHA = 1.702
MOE_ROWS_LOG2 = 8
MOE_ROWS = 1 << MOE_ROWS_LOG2

NN = (((1,), (0,)), ((), ()))
NT = (((1,), (1,)), ((), ()))


def _cparams(sem):
    return pltpu.CompilerParams(dimension_semantics=sem, vmem_limit_bytes=VMEM_LIMIT)


def _bdot(a, b, dims=NN):
    return lax.dot_general(a.astype(BF16), b.astype(BF16), dims, preferred_element_type=F32)


def _parts(x, n):
    out, rem = [], x
    for i in range(n):
        h = rem.astype(BF16)
        out.append(h)
        if i + 1 < n:
            rem = rem - h.astype(F32)
    return out


def _dotx(a, b, na=2, nb=2, dims=NN):
    ap, bp = _parts(a, na), _parts(b, nb)
    acc = None
    for i, ai in enumerate(ap):
        for j, bj in enumerate(bp):
            if i + j < max(na, nb):
                t = lax.dot_general(ai, bj, dims, preferred_element_type=F32)
                acc = t if acc is None else acc + t
    return acc


def _layer_norm(v, g, b):
    mu = jnp.mean(v, axis=-1, keepdims=True)
    c = v - mu
    var = jnp.mean(c * c, axis=-1, keepdims=True)
    return c * lax.rsqrt(var + LN_EPS) * g + b


def _sigmoid(x):
    return jax.nn.sigmoid(x)


def _silu(x):
    return x * jax.nn.sigmoid(x)


def _const_spec(shape):
    return pl.BlockSpec(shape, lambda *_: (0,) * len(shape), pipeline_mode=pl.Buffered(1))


def _mm_kernel(x_ref, w_ref, o_ref):
    o_ref[...] = _bdot(x_ref[...], w_ref[0])


def matmul(x, w_stack, idx, tm=1024, tn=512):
    m, k = x.shape
    n = w_stack.shape[2]
    tm = min(tm, m)
    return pl.pallas_call(
        _mm_kernel,
        out_shape=jax.ShapeDtypeStruct((m, n), F32),
        grid=(pl.cdiv(m, tm), pl.cdiv(n, tn)),
        in_specs=[pl.BlockSpec((tm, k), lambda i, j: (i, 0)),
                  pl.BlockSpec((1, k, tn), lambda i, j: (idx, 0, j))],
        out_specs=pl.BlockSpec((tm, tn), lambda i, j: (i, j)),
        compiler_params=_cparams(("parallel", "arbitrary")),
        name="dense_matmul",
    )(x, w_stack)


def _mm_res_ln_kernel(n_in, *refs):
    a_refs = refs[:n_in]
    w_refs = refs[n_in:2 * n_in]
    x_ref, g_ref, b_ref, o_ref = refs[2 * n_in:]
    h = None
    for a_ref, w_ref in zip(a_refs, w_refs):
        t = _bdot(a_ref[...], w_ref[...])
        h = t if h is None else h + t
    o_ref[...] = _layer_norm(DEEPNORM_ALPHA * x_ref[...] + h, g_ref[...], b_ref[...])


def mm_res_ln(a_list, w, x, g, b, tm=256):
    m, d = x.shape
    tm = min(tm, m)
    in_specs, off = [], 0
    for a in a_list:
        in_specs.append(pl.BlockSpec((tm, a.shape[1]), lambda i: (i, 0)))
    w_parts = []
    for a in a_list:
        ka = a.shape[1]
        w_parts.append(lax.slice_in_dim(w, off, off + ka, axis=0))
        in_specs.append(_const_spec((ka, d)))
        off += ka
    in_specs += [pl.BlockSpec((tm, d), lambda i: (i, 0)), _const_spec((1, d)), _const_spec((1, d))]
    return pl.pallas_call(
        functools.partial(_mm_res_ln_kernel, len(a_list)),
        out_shape=jax.ShapeDtypeStruct((m, d), F32),
        grid=(m // tm,),
        in_specs=in_specs,
        out_specs=pl.BlockSpec((tm, d), lambda i: (i, 0)),
        compiler_params=_cparams(("parallel",)),
        name="proj_residual_layernorm",
    )(*a_list, *w_parts, x, g.reshape(1, d), b.reshape(1, d))


RWKV_STEP_LANES = 4 * LANES
GDN_STEP_LANES = 4 * LANES
GDN_BLOCK_LANES = 2 * LANES


def _group_sum(x, gmat):
    return _dotx(x, gmat, 2, 1)


def _rwkv_kernel(pr_ref, pk_ref, pv_ref, pl_ref, mur_ref, muk_ref, muv_ref, mul_ref,
                 w0_ref, w2_ref, a0_ref, a2_ref, g2_ref, kk_ref, ka_ref, rk_ref, gng_ref, gnb_ref,
                 o_ref, prev_ref, prevl_ref, state_ref):
    c = pl.program_id(1)
    nb, ch, wd = pr_ref.shape

    @pl.when(c == 0)
    def _():
        prev_ref[...] = jnp.zeros_like(prev_ref)
        prevl_ref[...] = jnp.zeros_like(prevl_ref)
        state_ref[...] = jnp.zeros_like(state_ref)

    col = lax.broadcasted_iota(jnp.int32, (ch, LANES), 1)
    mlo = (col < RWKV_HEAD).astype(F32)
    mhi = 1.0 - mlo
    r2 = lax.broadcasted_iota(jnp.int32, (2 * ch, 2 * ch), 0)
    c2 = lax.broadcasted_iota(jnp.int32, (2 * ch, 2 * ch), 1)
    same = (r2 // ch) == (c2 // ch)
    strict = same & ((r2 % ch) > (c2 % ch))
    incl = same & ((r2 % ch) >= (c2 % ch))
    eye = (r2 == c2).astype(F32)
    rg = lax.broadcasted_iota(jnp.int32, (wd, wd), 0)
    cg = lax.broadcasted_iota(jnp.int32, (wd, wd), 1)
    gmat = ((rg // RWKV_HEAD) == (cg // RWKV_HEAD)).astype(BF16)
    rr = lax.broadcasted_iota(jnp.int32, (ch, ch), 0)
    cc = lax.broadcasted_iota(jnp.int32, (ch, ch), 1)
    tri = (rr >= cc).astype(BF16)

    def shifted(x, prev):
        return jnp.where(lax.broadcasted_iota(jnp.int32, x.shape, 0) == 0, prev, pltpu.roll(x, 1, 0))

    def stack(x):
        return jnp.concatenate([x * mlo, x * mhi], axis=0)

    pre = []
    for b in range(nb):
        raw = [pr_ref[b], pk_ref[b], pv_ref[b]]
        rawl = pl_ref[b]
        mus = [mur_ref[...], muk_ref[...], muv_ref[...]]
        mixed = []
        for i in range(3):
            xs = shifted(raw[i], prev_ref[b, i])
            mixed.append(raw[i] + (xs - raw[i]) * mus[i])
        xsl = shifted(rawl, prevl_ref[b])
        lora = rawl + (xsl - rawl) * mul_ref[...]
        for i in range(3):
            prev_ref[b, i] = raw[i][ch - 1:ch]
        prevl_ref[b] = rawl[ch - 1:ch]
        pre.append((mixed, lora))

    elem = []
    for b in range(nb):
        (r, k, v), lora = pre[b]
        l1, l2 = lora[:, :LANES], lora[:, LANES:]
        w = w0_ref[...] + _bdot(jnp.tanh(l1), w2_ref[...])
        ld = -math.exp(-0.5) * _sigmoid(w)
        a = _sigmoid(a0_ref[...] + _bdot(l1, a2_ref[...]))
        g = _bdot(_sigmoid(l2), g2_ref[...])
        kraw = k * kk_ref[...]
        kk = kraw * lax.rsqrt(_group_sum(kraw * kraw, gmat) + 1e-6)
        kmod = k * (1.0 + (a - 1.0) * ka_ref[...])
        cum = _dotx(tri, ld, 1, 3)
        w_t = jnp.exp(cum)
        w_prev = jnp.exp(cum - ld)
        w_inv = jnp.exp(-cum)
        w_end = jnp.exp(cum[ch - 1:ch])
        beta = kk * a
        fb = beta * w_inv
        fk = kmod * w_inv
        elem.append(dict(r=r, v=v, g=g, kmod=kmod, w_end=w_end, fa=-kk * w_prev, fb=fb, fk=fk, fq=r * w_t,
                         fbh=fb * w_end, fkh=fk * w_end))

    chains = [(b, hp) for b in range(nb) for hp in range(wd // LANES)]
    lane = lambda hp: slice(hp * LANES, (hp + 1) * LANES)
    take = lambda name: [stack(elem[b][name][:, lane(hp)]) for b, hp in chains]
    a2, b2, k2, q2, v2, bh2, kh2 = (take(n) for n in ("fa", "fb", "fk", "fq", "v", "fbh", "fkh"))
    lab = [jnp.where(strict, _bdot(x, y, dims=NT), 0.0) for x, y in zip(a2, b2)]
    lak = [jnp.where(strict, _bdot(x, y, dims=NT), 0.0) for x, y in zip(a2, k2)]
    grb = [jnp.where(incl, _bdot(x, y, dims=NT), 0.0) for x, y in zip(q2, b2)]
    grk = [jnp.where(incl, _bdot(x, y, dims=NT), 0.0) for x, y in zip(q2, k2)]
    minv = [eye + x for x in lab]
    pw = lab
    for _ in range(5):
        pw = [_bdot(x, x) for x in pw]
        minv = [m + _bdot(m, x) for m, x in zip(minv, pw)]
    t0 = [state_ref[b, hp] for b, hp in chains]
    rhs = [_bdot(x, t) + _bdot(l, vv) for x, t, l, vv in zip(a2, t0, lak, v2)]
    u2 = [_bdot(m, x) for m, x in zip(minv, rhs)]
    y2 = [_bdot(q, t) + _bdot(gb, u) + _bdot(gk, vv) for q, t, gb, u, gk, vv in zip(q2, t0, grb, u2, grk, v2)]
    for ci, (b, hp) in enumerate(chains):
        state_ref[b, hp] = (elem[b]["w_end"][:, lane(hp)].T * t0[ci] + _bdot(bh2[ci].T, u2[ci])
                            + _bdot(kh2[ci].T, v2[ci]))

    for b in range(nb):
        e = elem[b]
        y = jnp.concatenate([y2[ci][:ch] + y2[ci][ch:] for ci, (bb, hp) in enumerate(chains) if bb == b], axis=1)
        inv_n = 1.0 / RWKV_HEAD
        mean = _group_sum(y, gmat) * inv_n
        yc = y - mean
        var = _group_sum(yc * yc, gmat) * inv_n
        yn = yc * lax.rsqrt(var + RWKV_GN_EPS) * gng_ref[...] + gnb_ref[...]
        bonus = _group_sum(e["r"] * e["kmod"] * rk_ref[...], gmat) * e["v"]
        o_ref[b] = (yn + bonus) * e["g"]


def rwkv_mix(p3, mu, w0, w2, a0, a2, g2, k_k, k_a, r_k, gn_g, gn_b):
    nb, s, _ = p3.shape
    width = w0.shape[0]
    wd = RWKV_STEP_LANES
    nstep = width // wd
    rank_w, rank_a = w2.shape[0], a2.shape[0]
    assert rank_w + rank_a == LANES and g2.shape[0] == LANES and 2 * CHUNK == LANES and 2 * RWKV_HEAD == LANES
    assert width % wd == 0 and (3 * width) % (2 * LANES) == 0
    w2p = jnp.concatenate([w2, jnp.zeros((rank_a, width), F32)], 0)
    a2p = jnp.concatenate([jnp.zeros((rank_w, width), F32), a2], 0)
    row = lambda t: t.reshape(1, -1)
    blk = lambda off: pl.BlockSpec((nb, CHUNK, wd), lambda j, c: (0, c, off + j))
    vec = lambda off: pl.BlockSpec((1, wd), lambda j, c: (0, off + j))
    mat = pl.BlockSpec((LANES, wd), lambda j, c: (0, j))
    lora_blk = 3 * width // (2 * LANES)
    mu2 = row(mu)
    return pl.pallas_call(
        _rwkv_kernel,
        out_shape=jax.ShapeDtypeStruct((nb, s, width), F32),
        grid=(nstep, s // CHUNK),
        in_specs=[blk(0), blk(nstep), blk(2 * nstep),
                  pl.BlockSpec((nb, CHUNK, 2 * LANES), lambda j, c: (0, c, lora_blk)),
                  vec(0), vec(nstep), vec(2 * nstep),
                  pl.BlockSpec((1, 2 * LANES), lambda j, c: (0, lora_blk)),
                  vec(0), mat, vec(0), mat, mat, vec(0), vec(0), vec(0), vec(0), vec(0)],
        out_specs=pl.BlockSpec((nb, CHUNK, wd), lambda j, c: (0, c, j)),
        scratch_shapes=[pltpu.VMEM((nb, 3, 1, wd), F32),
                        pltpu.VMEM((nb, 1, 2 * LANES), F32),
                        pltpu.VMEM((nb, wd // LANES, LANES, LANES), F32)],
        compiler_params=_cparams(("parallel", "arbitrary")),
        name="rwkv7_mixer",
    )(p3, p3, p3, p3, mu2, mu2, mu2, mu2, row(w0), w2p, row(a0), a2p, g2,
      row(k_k), row(k_a), row(r_k), row(gn_g), row(gn_b))


def _gdn_kernel(nsub, *refs):
    pq_refs, pk_refs, pv_refs, pg_refs = (refs[i * nsub:(i + 1) * nsub] for i in range(4))
    pbd_ref, cq_ref, ck_ref, cv_ref, alog_ref, dtb_ref, ng_ref, o_ref, halo_ref, state_ref = refs[4 * nsub:]
    j = pl.program_id(0)
    c = pl.program_id(1)
    nb, ch, sub = pq_refs[0].shape
    wd = nsub * sub
    nheads = alog_ref.shape[1]
    hps = wd // GDN_HEAD

    @pl.when(c == 0)
    def _():
        halo_ref[...] = jnp.zeros_like(halo_ref)
        state_ref[...] = jnp.zeros_like(state_ref)

    col = lax.broadcasted_iota(jnp.int32, (ch, LANES), 1)
    rr = lax.broadcasted_iota(jnp.int32, (ch, ch), 0)
    cc = lax.broadcasted_iota(jnp.int32, (ch, ch), 1)
    causal = rr >= cc
    strict = rr > cc
    tri = causal.astype(BF16)
    triu = (rr <= cc).astype(BF16)
    eye = (rr == cc).astype(F32)
    ones = jnp.ones((LANES, LANES), BF16)
    hcol = lax.broadcasted_iota(jnp.int32, (1, nheads), 1)

    def conv_silu(x_refs, w_ref, b, i):
        for si, x_ref in enumerate(x_refs):
            halo_ref[b, i, 8:, si * sub:(si + 1) * sub] = x_ref[b]
        acc = None
        for t in range(GDN_CONV):
            lo = 8 - (GDN_CONV - 1) + t
            term = halo_ref[b, i, lo:lo + ch, :] * w_ref[t:t + 1, :]
            acc = term if acc is None else acc + term
        halo_ref[b, i, 0:8, :] = halo_ref[b, i, ch:ch + 8, :]
        return _silu(acc)

    conv = [(conv_silu(pq_refs, cq_ref, b, 0), conv_silu(pk_refs, ck_ref, b, 1), conv_silu(pv_refs, cv_ref, b, 2))
            for b in range(nb)]
    chains = [(b, hh) for b in range(nb) for hh in range(hps)]
    lane = lambda hh: slice(hh * GDN_HEAD, (hh + 1) * GDN_HEAD)
    q = [conv[b][0][:, lane(hh)] for b, hh in chains]
    k = [conv[b][1][:, lane(hh)] for b, hh in chains]
    v = [conv[b][2][:, lane(hh)] for b, hh in chains]
    q = [x * lax.rsqrt(_dotx(x * x, ones, 2, 1) + 1e-6) * (GDN_HEAD ** -0.5) for x in q]
    k = [x * lax.rsqrt(_dotx(x * x, ones, 2, 1) + 1e-6) for x in k]
    beta, la = [], []
    for b, hh in chains:
        h = j * hps + hh
        bd = pbd_ref[b]
        a_coef = -jnp.exp(jnp.sum(jnp.where(hcol == h, alog_ref[...], 0.0), axis=-1, keepdims=True))
        dt_b = jnp.sum(jnp.where(hcol == h, dtb_ref[...], 0.0), axis=-1, keepdims=True)
        bcol = jnp.sum(jnp.where(col == h, bd, 0.0), axis=-1, keepdims=True)
        dcol = jnp.sum(jnp.where(col == h + nheads, bd, 0.0), axis=-1, keepdims=True)
        beta.append(_sigmoid(bcol))
        z = dcol + dt_b
        softplus = jnp.maximum(z, 0.0) + jnp.log(1.0 + jnp.exp(-jnp.abs(z)))
        la.append(jnp.broadcast_to(a_coef * softplus, (ch, LANES)))
    gc = [_dotx(tri, x, 1, 3) for x in la]
    gc_row = [_dotx(jnp.broadcast_to(x.T[0:1, :], (ch, ch)), triu, 3, 1) for x in la]
    decay = [jnp.where(causal, jnp.exp(jnp.where(causal, g[:, :ch] - gr, 0.0)), 0.0) for g, gr in zip(gc, gc_row)]
    eg = [jnp.exp(g) for g in gc]
    kb = [x * bt for x, bt in zip(k, beta)]
    lower = [jnp.where(strict, _bdot(x, y, dims=NT) * dc, 0.0) for x, y, dc in zip(kb, k, decay)]
    tinv = [eye - x for x in lower]
    pw = [-x for x in lower]
    for _ in range(5):
        pw = [_bdot(x, x) for x in pw]
        tinv = [t + _bdot(t, x) for t, x in zip(tinv, pw)]
    u = [_bdot(t, x * bt) for t, x, bt in zip(tinv, v, beta)]
    w = [_bdot(t, x * e) for t, x, e in zip(tinv, kb, eg)]
    qk = [_bdot(x, y, dims=NT) * dc for x, y, dc in zip(q, k, decay)]
    k_dec = [x * jnp.exp(g[ch - 1:ch, :] - g) for x, g in zip(k, gc)]
    t0 = [state_ref[b, hh] for b, hh in chains]
    v_new = [x - _bdot(y, t) for x, y, t in zip(u, w, t0)]
    out = [_bdot(x * e, t) + _bdot(y, vn) for x, e, t, y, vn in zip(q, eg, t0, qk, v_new)]
    for ci, (b, hh) in enumerate(chains):
        state_ref[b, hh] = t0[ci] * eg[ci][ch - 1:ch, 0:1] + _bdot(k_dec[ci].T, v_new[ci])
    for ci, (b, hh) in enumerate(chains):
        ms = _dotx(out[ci] * out[ci], ones, 2, 1) * (1.0 / GDN_HEAD)
        o = out[ci] * lax.rsqrt(ms + 1e-6) * ng_ref[...]
        si, lo = divmod(hh * GDN_HEAD, sub)
        o_ref[b, :, lane(hh)] = o * _silu(pg_refs[si][b, :, lo:lo + GDN_HEAD])


def gdn_mix(p3, col0, conv_w, a_log, dt_bias, norm_g):
    nb, s, _ = p3.shape
    nheads = a_log.shape[0]
    width = nheads * GDN_HEAD
    wd, sub = GDN_STEP_LANES, GDN_BLOCK_LANES
    nstep, nsub = width // wd, wd // sub
    assert col0 % sub == 0 and width % wd == 0 and wd % sub == 0 and GDN_HEAD == LANES

    def blks(col):
        base = col // sub
        return [pl.BlockSpec((nb, CHUNK, sub), lambda j, c, o=base + si: (0, c, o + j * nsub)) for si in range(nsub)]

    cw = pl.BlockSpec((GDN_CONV, wd), lambda j, c: (0, j))
    small = lambda n: pl.BlockSpec((1, n), lambda j, c: (0, 0))
    return pl.pallas_call(
        functools.partial(_gdn_kernel, nsub),
        out_shape=jax.ShapeDtypeStruct((nb, s, width), F32),
        grid=(nstep, s // CHUNK),
        in_specs=blks(col0) + blks(col0 + width) + blks(col0 + 2 * width) + blks(col0 + 3 * width) + [
            pl.BlockSpec((nb, CHUNK, LANES), lambda j, c: (0, c, (col0 + 4 * width) // LANES)),
            cw, cw, cw, small(nheads), small(nheads), small(LANES)],
        out_specs=pl.BlockSpec((nb, CHUNK, wd), lambda j, c: (0, c, j)),
        scratch_shapes=[pltpu.VMEM((nb, 3, 8 + CHUNK, wd), F32),
                        pltpu.VMEM((nb, wd // GDN_HEAD, LANES, LANES), F32)],
        compiler_params=_cparams(("parallel", "arbitrary")),
        name="gated_deltanet_mixer",
    )(*([p3] * (4 * nsub + 1)), conv_w[:, :width], conv_w[:, width:2 * width], conv_w[:, 2 * width:],
      a_log.reshape(1, -1), dt_bias.reshape(1, -1), norm_g.reshape(1, -1))


HALO_U = 32
HALO_Z = 16
CONV_TILE_ROWS = 128


def _odd_kernel(p_ref, cvw_ref, cvb_ref, lng_ref, lnb_ref, plw_ref, pls_ref, o_ref, ubuf_ref, zbuf_ref, conv_ref):
    s_idx = pl.program_id(1)
    ts = p_ref.shape[1]
    cw = cvb_ref.shape[1]
    pool_w = pls_ref.shape[1]
    pg = pool_w // len(POOL_WINDOWS)

    @pl.when(s_idx == 0)
    def _():
        ubuf_ref[0:HALO_U, :] = jnp.zeros((HALO_U, cw), F32)
        zbuf_ref[0:HALO_Z, :] = jnp.zeros((HALO_Z, pool_w), F32)

    pa = p_ref[0, :, 0:cw]
    pb = p_ref[0, :, cw:2 * cw]
    ubuf_ref[HALO_U:, :] = pa * _sigmoid(pb)
    base = HALO_U - (CONV_KERNEL - 1)
    for r0 in range(0, ts, CONV_TILE_ROWS):
        for c0 in range(0, cw, LANES):
            acc = jnp.broadcast_to(cvb_ref[:, c0:c0 + LANES], (CONV_TILE_ROWS, LANES))
            for j in range(CONV_KERNEL):
                lo = base + j + r0
                acc = acc + ubuf_ref[lo:lo + CONV_TILE_ROWS, c0:c0 + LANES] * cvw_ref[j:j + 1, c0:c0 + LANES]
            conv_ref[r0:r0 + CONV_TILE_ROWS, c0:c0 + LANES] = acc
    ubuf_ref[0:HALO_U, :] = ubuf_ref[ts:ts + HALO_U, :]
    o_ref[0, :, 0:cw] = _silu(_layer_norm(conv_ref[...], lng_ref[...], lnb_ref[...]))

    z = p_ref[0, :, 2 * cw:]
    zbuf_ref[HALO_Z:, :] = z
    t1 = (s_idx * ts + 1 + lax.broadcasted_iota(jnp.int32, (ts, 1), 0)).astype(F32)
    for gi, win in enumerate(POOL_WINDOWS):
        lo = gi * pg
        ssum = None
        for j in range(win):
            t = zbuf_ref[HALO_Z - j:HALO_Z - j + ts, lo:lo + pg]
            ssum = t if ssum is None else ssum + t
        pooled = ssum / jnp.minimum(t1, float(win)) - z[:, lo:lo + pg]
        mixed = _bdot(pooled, plw_ref[gi]) * pls_ref[:, lo:lo + pg]
        o_ref[0, :, cw + lo:cw + lo + pg] = mixed
    zbuf_ref[0:HALO_Z, :] = zbuf_ref[ts:ts + HALO_Z, :]


def odd_mix(p3, cv_w, cv_b, cv_ln_g, cv_ln_b, pl_w, pl_scale, ts=256):
    nb, s, cols = p3.shape
    cw = cv_b.shape[0]
    pool_w = pl_scale.shape[0]
    ts = min(ts, s)
    assert ts % CONV_TILE_ROWS == 0
    row = lambda t: t.reshape(1, -1)
    cvw = jnp.concatenate([cv_w, jnp.zeros((HALO_U - CONV_KERNEL, cw), F32)], 0)
    return pl.pallas_call(
        _odd_kernel,
        out_shape=jax.ShapeDtypeStruct((nb, s, cw + pool_w), F32),
        grid=(nb, s // ts),
        in_specs=[pl.BlockSpec((1, ts, cols), lambda b, i: (b, i, 0)),
                  _const_spec((HALO_U, cw)), _const_spec((1, cw)), _const_spec((1, cw)), _const_spec((1, cw)),
                  _const_spec(pl_w.shape), _const_spec((1, pool_w))],
        out_specs=pl.BlockSpec((1, ts, cw + pool_w), lambda b, i: (b, i, 0)),
        scratch_shapes=[pltpu.VMEM((HALO_U + ts, cw), F32), pltpu.VMEM((HALO_Z + ts, pool_w), F32),
                        pltpu.VMEM((ts, cw), F32)],
        compiler_params=_cparams(("parallel", "arbitrary")),
        name="conformer_pool_mixer",
    )(p3, cvw, row(cv_b), row(cv_ln_g), row(cv_ln_b), pl_w.astype(BF16), row(pl_scale))


def _xattn_kernel(x_ref, k_ref, v_ref, wq_ref, wo_ref, g_ref, b_ref, o_ref):
    x = x_ref[0]
    d = x.shape[1]
    dh = d // XA_HEADS
    q = _bdot(x, wq_ref[...])
    heads = []
    for hd in range(XA_HEADS):
        sl = slice(hd * dh, (hd + 1) * dh)
        s = _bdot(q[:, sl], k_ref[0, :, sl], NT) * (dh ** -0.5)
        s = s - jnp.max(s, axis=-1, keepdims=True)
        e = jnp.exp(s)
        prob = e / jnp.sum(e, axis=-1, keepdims=True)
        heads.append(_bdot(prob, v_ref[0, :, sl]))
    a = _bdot(jnp.concatenate(heads, axis=-1), wo_ref[...])
    o_ref[0] = _layer_norm(DEEPNORM_ALPHA * x + a, g_ref[...], b_ref[...])


def cross_attention_ln(x3, k3, v3, wq, wo, g, b, ts=256):
    nb, s, d = x3.shape
    m = k3.shape[1]
    ts = min(ts, s)
    return pl.pallas_call(
        _xattn_kernel,
        out_shape=jax.ShapeDtypeStruct((nb, s, d), F32),
        grid=(nb, s // ts),
        in_specs=[pl.BlockSpec((1, ts, d), lambda bi, i: (bi, i, 0)),
                  pl.BlockSpec((1, m, d), lambda bi, i: (bi, 0, 0)),
                  pl.BlockSpec((1, m, d), lambda bi, i: (bi, 0, 0)),
                  _const_spec((d, d)), _const_spec((d, d)), _const_spec((1, d)), _const_spec((1, d))],
        out_specs=pl.BlockSpec((1, ts, d), lambda bi, i: (bi, i, 0)),
        compiler_params=_cparams(("parallel", "parallel")),
        name="cross_attention_layernorm",
    )(x3, k3, v3, wq, wo, g.reshape(1, d), b.reshape(1, d))


def _router_kernel(x_ref, wr_ref, br_ref, idx_ref, gate_ref, rank_ref, cnt_ref, base_ref, tri_ref):
    i = pl.program_id(0)
    tm = x_ref.shape[0]
    ne = wr_ref.shape[0]

    @pl.when(i == 0)
    def _():
        base_ref[...] = jnp.zeros_like(base_ref)
        rr = lax.broadcasted_iota(jnp.int32, (tm, tm), 0)
        cc = lax.broadcasted_iota(jnp.int32, (tm, tm), 1)
        tri_ref[...] = (rr <= cc).astype(BF16)

    logits = _dotx(wr_ref[...], x_ref[...], dims=NT) + br_ref[...]
    eidx = lax.broadcasted_iota(jnp.int32, (ne, tm), 0)
    work = logits
    tops, sels = [], []
    for kk in range(TOP_K):
        mx = jnp.max(work, axis=0, keepdims=True)
        sel_idx = jnp.min(jnp.where(work == mx, eidx, ne), axis=0, keepdims=True)
        sel = eidx == sel_idx
        tops.append(mx)
        sels.append(sel)
        idx_ref[kk:kk + 1, :] = sel_idx
        work = jnp.where(sel, -jnp.inf, work)
    es = [jnp.exp(t - tops[0]) for t in tops]
    den = es[0] + es[1] + es[2] + es[3]
    for kk in range(TOP_K):
        gate_ref[kk:kk + 1, :] = es[kk] / den
    onehot = sels[0] | sels[1] | sels[2] | sels[3]
    oh = jnp.where(onehot, 1.0, 0.0)
    incl = jnp.dot(oh.astype(BF16), tri_ref[...], preferred_element_type=F32)
    before = base_ref[:, 0:1] + incl - oh
    for kk in range(TOP_K):
        rank = jnp.sum(jnp.where(sels[kk], before, 0.0), axis=0, keepdims=True)
        rank_ref[kk:kk + 1, :] = rank.astype(jnp.int32)
    total = base_ref[:, 0:1] + incl[:, tm - 1:tm]
    base_ref[...] = jnp.broadcast_to(total, base_ref.shape)
    cnt_ref[...] = jnp.broadcast_to(total, cnt_ref.shape).astype(jnp.int32)


def moe_route(xf, w_r, b_r, tm=512):
    n, d = xf.shape
    ne = w_r.shape[1]
    tm = min(tm, n)
    slot = lambda dt: jax.ShapeDtypeStruct((TOP_K, n), dt)
    return pl.pallas_call(
        _router_kernel,
        out_shape=(slot(jnp.int32), slot(F32), slot(jnp.int32), jax.ShapeDtypeStruct((ne, LANES), jnp.int32)),
        grid=(n // tm,),
        in_specs=[pl.BlockSpec((tm, d), lambda i: (i, 0)), _const_spec((ne, d)), _const_spec((ne, 1))],
        out_specs=(pl.BlockSpec((TOP_K, tm), lambda i: (0, i)), pl.BlockSpec((TOP_K, tm), lambda i: (0, i)),
                   pl.BlockSpec((TOP_K, tm), lambda i: (0, i)), pl.BlockSpec((ne, LANES), lambda i: (0, 0))),
        scratch_shapes=[pltpu.VMEM((ne, LANES), F32), pltpu.VMEM((tm, tm), BF16)],
        compiler_params=_cparams(("arbitrary",)),
        name="moe_router",
    )(xf, w_r.T, b_r.reshape(ne, 1))


def _dispatch_kernel(dest_ref, npad_ref, pstart_ref, nused_ref, x_ref, zero_ref, xs_ref, sem, zsem):
    i = pl.program_id(0)
    tm = x_ref.shape[0]
    ne = npad_ref.shape[0]
    nblk = xs_ref.shape[0] // MOE_ROWS

    @pl.when(i == 0)
    def _():
        def per_expert(e, carry):
            def one(r, c2):
                pltpu.make_async_copy(zero_ref.at[pl.ds(0, 1)], xs_ref.at[pl.ds(pstart_ref[e] + r, 1)], zsem).start()
                return c2
            lax.fori_loop(0, npad_ref[e], one, 0)

            def one_wait(r, c2):
                pltpu.make_async_copy(zero_ref.at[pl.ds(0, 1)], xs_ref.at[pl.ds(0, 1)], zsem).wait()
                return c2
            lax.fori_loop(0, npad_ref[e], one_wait, 0)
            return carry
        lax.fori_loop(0, ne, per_expert, 0)

        def unused(blk, carry):
            cp = pltpu.make_async_copy(zero_ref, xs_ref.at[pl.ds(blk * MOE_ROWS, MOE_ROWS)], zsem)
            cp.start()
            cp.wait()
            return carry
        lax.fori_loop(nused_ref[0], nblk, unused, 0)

    def issue(t, carry):
        for kk in range(TOP_K):
            pltpu.make_async_copy(x_ref.at[pl.ds(t, 1)], xs_ref.at[pl.ds(dest_ref[kk, t], 1)], sem).start()
        return carry
    lax.fori_loop(0, tm, issue, 0)

    def drain(t, carry):
        for kk in range(TOP_K):
            pltpu.make_async_copy(x_ref.at[pl.ds(0, 1)], xs_ref.at[pl.ds(0, 1)], sem).wait()
        return carry
    lax.fori_loop(0, tm, drain, 0)


def moe_dispatch(xf, dest, npad, pad_fill_start, nused, cap, tm=256):
    n, d = xf.shape
    tm = min(tm, n)
    smem = pl.BlockSpec(memory_space=pltpu.SMEM)
    return pl.pallas_call(
        _dispatch_kernel,
        out_shape=jax.ShapeDtypeStruct((cap, d), F32),
        grid=(n // tm,),
        in_specs=[pl.BlockSpec((TOP_K, tm), lambda i: (0, i), memory_space=pltpu.SMEM), smem, smem, smem,
                  pl.BlockSpec((tm, d), lambda i: (i, 0)), _const_spec((MOE_ROWS, d))],
        out_specs=pl.BlockSpec(memory_space=pl.ANY),
        scratch_shapes=[pltpu.SemaphoreType.DMA, pltpu.SemaphoreType.DMA],
        compiler_params=_cparams(("arbitrary",)),
        name="moe_dispatch",
    )(dest, npad, pad_fill_start, nused, xf, jnp.zeros((MOE_ROWS, d), F32))


PAIR_TILE = 2 * LANES


def _expert_changed(be_ref, i):
    return (i == 0) | (be_ref[i] != be_ref[jnp.maximum(i - 1, 0)])


def _next_weights(be_ref, nxt_ref, i, layer, w_hbm, wbuf_ref, sem, use):
    copy = lambda e: pltpu.make_async_copy(w_hbm.at[layer, e], wbuf_ref, sem)
    e = be_ref[i]

    @pl.when(i == 0)
    def _():
        copy(e).start()

    copy(e).wait()
    use()
    nx = nxt_ref[e]

    @pl.when(nx >= 0)
    def _():
        copy(nx).start()


def _expert_gu_kernel(layer, be_ref, nused_ref, nxt_ref, xs_ref, wgu_hbm, bgu_ref, act_ref,
                      wbuf_ref, wperm_ref, bperm_ref, sem):
    i = pl.program_id(0)
    active = i < nused_ref[0]
    ntile = wbuf_ref.shape[1] // PAIR_TILE

    def deinterleave():
        r = lax.broadcasted_iota(jnp.int32, (PAIR_TILE, PAIR_TILE), 0)
        c = lax.broadcasted_iota(jnp.int32, (PAIR_TILE, PAIR_TILE), 1)
        perm = (r == jnp.where(c < LANES, 2 * c, 2 * (c - LANES) + 1)).astype(BF16)
        for t in range(ntile):
            sl = slice(t * PAIR_TILE, (t + 1) * PAIR_TILE)
            w = wbuf_ref[:, sl].astype(BF16)
            wperm_ref[:, sl] = jnp.dot(w, perm, preferred_element_type=F32).astype(BF16)
            bias = jnp.broadcast_to(bgu_ref[0, 0, :, sl], (8, PAIR_TILE))
            bperm_ref[:, sl] = _dotx(bias, perm, 3, 1)

    @pl.when(active & _expert_changed(be_ref, i))
    def _():
        _next_weights(be_ref, nxt_ref, i, layer, wgu_hbm, wbuf_ref, sem, deinterleave)

    @pl.when(active)
    def _():
        x = xs_ref[...].astype(BF16)
        for t in range(ntile):
            sl = slice(t * PAIR_TILE, (t + 1) * PAIR_TILE)
            h = jnp.dot(x, wperm_ref[:, sl], preferred_element_type=F32) + bperm_ref[0:1, sl]
            gate = jnp.minimum(h[:, :LANES], SWIGLU_LIMIT)
            up = jnp.clip(h[:, LANES:], -SWIGLU_LIMIT, SWIGLU_LIMIT)
            act = (up + 1.0) * gate * _sigmoid(SWIGLU_ALPHA * gate)
            act_ref[:, t * LANES:(t + 1) * LANES] = act.astype(BF16)

    @pl.when(jnp.logical_not(active))
    def _():
        act_ref[...] = jnp.zeros_like(act_ref)


def _expert_dn_kernel(layer, be_ref, nused_ref, nxt_ref, act_ref, wdn_hbm, bdn_ref, y_ref, wbuf_ref, wd_ref, sem):
    i = pl.program_id(0)
    active = i < nused_ref[0]

    def cast():
        wd_ref[...] = wbuf_ref[...].astype(BF16)

    @pl.when(active & _expert_changed(be_ref, i))
    def _():
        _next_weights(be_ref, nxt_ref, i, layer, wdn_hbm, wbuf_ref, sem, cast)

    @pl.when(active)
    def _():
        y_ref[...] = jnp.dot(act_ref[...], wd_ref[...], preferred_element_type=F32) + bdn_ref[0, 0]

    @pl.when(jnp.logical_not(active))
    def _():
        y_ref[...] = jnp.zeros_like(y_ref)


def moe_experts(xs, block_expert, nused, next_expert, layer, w_gu, b_gu, w_dn, b_dn):
    cap, d = xs.shape
    nl, ne, _, d2 = w_gu.shape
    dff = d2 // 2
    nblk = cap // MOE_ROWS
    blk = lambda i, be, nu, nx: (i, 0)
    bias = lambda n: pl.BlockSpec((1, 1, 1, n), lambda i, be, nu, nx: (layer, be[i], 0, 0))
    hbm = pl.BlockSpec(memory_space=pl.ANY)
    act = pl.pallas_call(
        functools.partial(_expert_gu_kernel, layer),
        out_shape=jax.ShapeDtypeStruct((cap, dff), BF16),
        grid_spec=pltpu.PrefetchScalarGridSpec(
            num_scalar_prefetch=3, grid=(nblk,),
            in_specs=[pl.BlockSpec((MOE_ROWS, d), blk), hbm, bias(d2)],
            out_specs=pl.BlockSpec((MOE_ROWS, dff), blk),
            scratch_shapes=[pltpu.VMEM((d, d2), F32), pltpu.VMEM((d, d2), BF16), pltpu.VMEM((8, d2), F32),
                            pltpu.SemaphoreType.DMA]),
        compiler_params=_cparams(("arbitrary",)),
        name="moe_expert_gate_up",
    )(block_expert, nused, next_expert, xs, w_gu, b_gu.reshape(nl, ne, 1, d2))
    return pl.pallas_call(
        functools.partial(_expert_dn_kernel, layer),
        out_shape=jax.ShapeDtypeStruct((cap, d), F32),
        grid_spec=pltpu.PrefetchScalarGridSpec(
            num_scalar_prefetch=3, grid=(nblk,),
            in_specs=[pl.BlockSpec((MOE_ROWS, dff), blk), hbm, bias(d)],
            out_specs=pl.BlockSpec((MOE_ROWS, d), blk),
            scratch_shapes=[pltpu.VMEM((dff, d), F32), pltpu.VMEM((dff, d), BF16), pltpu.SemaphoreType.DMA]),
        compiler_params=_cparams(("arbitrary",)),
        name="moe_expert_down",
    )(block_expert, nused, next_expert, act, w_dn, b_dn.reshape(nl, ne, 1, d))


def _plan_kernel(idx_ref, rank_ref, cnt_ref, dest_ref):
    ne = cnt_ref.shape[0]
    tm = idx_ref.shape[1]
    padded = ((cnt_ref[...] + (MOE_ROWS - 1)) >> MOE_ROWS_LOG2) << MOE_ROWS_LOG2
    r = lax.broadcasted_iota(jnp.int32, (ne, ne), 0)
    c = lax.broadcasted_iota(jnp.int32, (ne, ne), 1)
    pad_start = _dotx((c < r).astype(BF16), padded.astype(F32), 1, 3)[:, 0:1]
    eidx = lax.broadcasted_iota(jnp.int32, (ne, tm), 0)
    for kk in range(TOP_K):
        start = jnp.sum(jnp.where(eidx == idx_ref[kk:kk + 1, :], pad_start, 0.0), axis=0, keepdims=True)
        dest_ref[kk:kk + 1, :] = rank_ref[kk:kk + 1, :] + start.astype(jnp.int32)


def moe_plan(idx, rank, cnt, tm=2048):
    k, n = idx.shape
    ne = cnt.shape[0]
    tm = min(tm, n)
    spec = pl.BlockSpec((k, tm), lambda i: (0, i))
    return pl.pallas_call(
        _plan_kernel,
        out_shape=jax.ShapeDtypeStruct((k, n), jnp.int32),
        grid=(n // tm,),
        in_specs=[spec, spec, _const_spec((ne, LANES))],
        out_specs=spec,
        compiler_params=_cparams(("parallel",)),
        name="moe_plan",
    )(idx, rank, cnt)


def _combine_kernel(dest_ref, y_ref, gate_ref, x_ref, g_ref, b_ref, o_ref, buf_ref, sem):
    tm = x_ref.shape[0]

    def issue(t, carry):
        for kk in range(TOP_K):
            pltpu.make_async_copy(y_ref.at[pl.ds(dest_ref[kk, t], 1)], buf_ref.at[kk, pl.ds(t, 1)], sem).start()
        return carry
    lax.fori_loop(0, tm, issue, 0)

    def drain(t, carry):
        for kk in range(TOP_K):
            pltpu.make_async_copy(y_ref.at[pl.ds(0, 1)], buf_ref.at[kk, pl.ds(0, 1)], sem).wait()
        return carry
    lax.fori_loop(0, tm, drain, 0)

    f = None
    for kk in range(TOP_K):
        t = buf_ref[kk] * gate_ref[:, kk:kk + 1]
        f = t if f is None else f + t
    o_ref[...] = _layer_norm(DEEPNORM_ALPHA * x_ref[...] + f, g_ref[...], b_ref[...])


def moe_combine_ln(y, dest, gate_t, xf, g, b, tm=256):
    n, d = xf.shape
    tm = min(tm, n)
    return pl.pallas_call(
        _combine_kernel,
        out_shape=jax.ShapeDtypeStruct((n, d), F32),
        grid=(n // tm,),
        in_specs=[pl.BlockSpec((TOP_K, tm), lambda i: (0, i), memory_space=pltpu.SMEM),
                  pl.BlockSpec(memory_space=pl.ANY),
                  pl.BlockSpec((tm, TOP_K), lambda i: (i, 0)),
                  pl.BlockSpec((tm, d), lambda i: (i, 0)), _const_spec((1, d)), _const_spec((1, d))],
        out_specs=pl.BlockSpec((tm, d), lambda i: (i, 0)),
        scratch_shapes=[pltpu.VMEM((TOP_K, tm, d), F32), pltpu.SemaphoreType.DMA],
        compiler_params=_cparams(("arbitrary",)),
        name="moe_combine_layernorm",
    )(dest, y, gate_t, xf, g.reshape(1, d), b.reshape(1, d))


def moe_ffn_ln(xf, layer, w_r, b_r, w_gu, b_gu, w_dn, b_dn, g, b):
    n, d = xf.shape
    ne = w_r.shape[1]
    idx, gate, rank, cnt = moe_route(xf, w_r, b_r)
    dest = moe_plan(idx, rank, cnt)
    counts = cnt[:, 0]
    padded = -(-counts // MOE_ROWS) * MOE_ROWS
    pad_end = jnp.cumsum(padded)
    pad_start = pad_end - padded
    cap = n * TOP_K + ne * MOE_ROWS
    nblk = cap // MOE_ROWS
    nused = (pad_end[-1] // MOE_ROWS).astype(jnp.int32).reshape(1)
    blk_row = jnp.arange(nblk, dtype=jnp.int32) * MOE_ROWS
    block_expert = jnp.minimum(jnp.sum(blk_row[:, None] >= pad_end[None, :], axis=1), ne - 1).astype(jnp.int32)
    xs = moe_dispatch(xf, dest, (padded - counts).astype(jnp.int32), (pad_start + counts).astype(jnp.int32),
                      nused, cap)
    eidx = jnp.arange(ne, dtype=jnp.int32)
    later = (eidx[None, :] > eidx[:, None]) & (counts[None, :] > 0)
    next_expert = jnp.min(jnp.where(later, eidx[None, :], ne), axis=1)
    next_expert = jnp.where(next_expert < ne, next_expert, -1).astype(jnp.int32)
    y = moe_experts(xs, block_expert, nused, next_expert, layer, w_gu, b_gu, w_dn, b_dn)
    return moe_combine_ln(y, dest, gate.T, xf, g, b)


def kernel(x, mem, ev_w_in, ev_mu, rk_w0, rk_w2, rk_a0, rk_a2, rk_g2, rk_kk, rk_ka, rk_rk, rk_gn_g, rk_gn_b,
           gd_conv, gd_a_log, gd_dt_bias, gd_norm_g, ev_w_out, od_w_in, cv_w, cv_b, cv_ln_g, cv_ln_b, pl_w,
           pl_scale, od_w_out, xa_wq, xa_wk, xa_wv, xa_wo, moe_wr, moe_br, moe_wgu, moe_bgu, moe_wdn, moe_bdn,
           ln_g, ln_b):
    nb, s, d = x.shape
    n = nb * s
    m = mem.shape[1]
    xf = x.reshape(n, d)
    memf = mem.reshape(nb * m, d)
    for layer in range(DEPTH):
        i = layer // 2
        if layer % 2 == 0:
            rwkv_cols = ev_mu.shape[1]
            p3 = matmul(xf, ev_w_in, i).reshape(nb, s, -1)
            ya = rwkv_mix(p3, ev_mu[i], rk_w0[i], rk_w2[i], rk_a0[i], rk_a2[i], rk_g2[i], rk_kk[i], rk_ka[i],
                          rk_rk[i].reshape(-1), rk_gn_g[i], rk_gn_b[i])
            yb = gdn_mix(p3, rwkv_cols, gd_conv[i], gd_a_log[i], gd_dt_bias[i], gd_norm_g[i])
            mixed = [ya.reshape(n, -1), yb.reshape(n, -1)]
            w_out = ev_w_out[i]
        else:
            p3 = matmul(xf, od_w_in, i).reshape(nb, s, -1)
            mixed = [odd_mix(p3, cv_w[i], cv_b[i], cv_ln_g[i], cv_ln_b[i], pl_w[i], pl_scale[i]).reshape(n, -1)]
            w_out = od_w_out[i]
        xf = mm_res_ln(mixed, w_out.astype(BF16), xf, ln_g[layer, 0], ln_b[layer, 0])
        k3 = matmul(memf, xa_wk, layer).reshape(nb, m, d)
        v3 = matmul(memf, xa_wv, layer).reshape(nb, m, d)
        xf = cross_attention_ln(xf.reshape(nb, s, d), k3, v3, xa_wq[layer].astype(BF16), xa_wo[layer].astype(BF16),
                                ln_g[layer, 1], ln_b[layer, 1]).reshape(n, d)
        xf = moe_ffn_ln(xf, layer, moe_wr[layer], moe_br[layer], moe_wgu, moe_bgu, moe_wdn, moe_bdn,
                        ln_g[layer, 2], ln_b[layer, 2])
    return xf.reshape(nb, s, d)
```

```python
import functools
import math

import jax
import jax.numpy as jnp
from jax import lax
from jax.experimental import pallas as pl
from jax.experimental.pallas import tpu as pltpu

F32 = jnp.float32
BF16 = jnp.bfloat16

LANES = 128
VMEM_LIMIT = 56 * 1024 * 1024

DEPTH = 2
DEEPNORM_ALPHA = (2 * DEPTH) ** 0.25
LN_EPS = 1e-5
CHUNK = 64
RWKV_HEAD = 64
RWKV_GN_EPS = 64e-5
GDN_HEAD = 128
GDN_CONV = 4
CONV_KERNEL = 31
POOL_WINDOWS = (2, 4, 8, 16)
XA_HEADS = 4
N_EXPERTS = 32
TOP_K = 4
SWIGLU_LIMIT = 7.0
SWIGLU_ALPHA = 1.702
MOE_ROWS_LOG2 = 8
MOE_ROWS = 1 << MOE_ROWS_LOG2

NN = (((1,), (0,)), ((), ()))
NT = (((1,), (1,)), ((), ()))


def _cparams(sem):
    return pltpu.CompilerParams(dimension_semantics=sem, vmem_limit_bytes=VMEM_LIMIT)


def _bdot(a, b, dims=NN):
    return lax.dot_general(a.astype(BF16), b.astype(BF16), dims, preferred_element_type=F32)


def _parts(x, n):
    out, rem = [], x
    for i in range(n):
        h = rem.astype(BF16)
        out.append(h)
        if i + 1 < n:
            rem = rem - h.astype(F32)
    return out


def _dotx(a, b, na=2, nb=2, dims=NN):
    ap, bp = _parts(a, na), _parts(b, nb)
    acc = None
    for i, ai in enumerate(ap):
        for j, bj in enumerate(bp):
            if i + j < max(na, nb):
                t = lax.dot_general(ai, bj, dims, preferred_element_type=F32)
                acc = t if acc is None else acc + t
    return acc


def _layer_norm(v, g, b):
    mu = jnp.mean(v, axis=-1, keepdims=True)
    c = v - mu
    var = jnp.mean(c * c, axis=-1, keepdims=True)
    return c * lax.rsqrt(var + LN_EPS) * g + b


def _sigmoid(x):
    return jax.nn.sigmoid(x)


def _silu(x):
    return x * jax.nn.sigmoid(x)


def _const_spec(shape):
    return pl.BlockSpec(shape, lambda *_: (0,) * len(shape), pipeline_mode=pl.Buffered(1))


def _mm_kernel(x_ref, w_ref, o_ref):
    o_ref[...] = _bdot(x_ref[...], w_ref[0])


def matmul(x, w_stack, idx, tm=1024, tn=512):
    m, k = x.shape
    n = w_stack.shape[2]
    tm = min(tm, m)
    return pl.pallas_call(
        _mm_kernel,
        out_shape=jax.ShapeDtypeStruct((m, n), F32),
        grid=(pl.cdiv(m, tm), pl.cdiv(n, tn)),
        in_specs=[pl.BlockSpec((tm, k), lambda i, j: (i, 0)),
                  pl.BlockSpec((1, k, tn), lambda i, j: (idx, 0, j))],
        out_specs=pl.BlockSpec((tm, tn), lambda i, j: (i, j)),
        compiler_params=_cparams(("parallel", "arbitrary")),
        name="dense_matmul",
    )(x, w_stack)


def _mm_res_ln_kernel(n_in, *refs):
    a_refs = refs[:n_in]
    w_refs = refs[n_in:2 * n_in]
    x_ref, g_ref, b_ref, o_ref = refs[2 * n_in:]
    h = None
    for a_ref, w_ref in zip(a_refs, w_refs):
        t = _bdot(a_ref[...], w_ref[...])
        h = t if h is None else h + t
    o_ref[...] = _layer_norm(DEEPNORM_ALPHA * x_ref[...] + h, g_ref[...], b_ref[...])


def mm_res_ln(a_list, w, x, g, b, tm=256):
    m, d = x.shape
    tm = min(tm, m)
    in_specs, off = [], 0
    for a in a_list:
        in_specs.append(pl.BlockSpec((tm, a.shape[1]), lambda i: (i, 0)))
    w_parts = []
    for a in a_list:
        ka = a.shape[1]
        w_parts.append(lax.slice_in_dim(w, off, off + ka, axis=0))
        in_specs.append(_const_spec((ka, d)))
        off += ka
    in_specs += [pl.BlockSpec((tm, d), lambda i: (i, 0)), _const_spec((1, d)), _const_spec((1, d))]
    return pl.pallas_call(
        functools.partial(_mm_res_ln_kernel, len(a_list)),
        out_shape=jax.ShapeDtypeStruct((m, d), F32),
        grid=(m // tm,),
        in_specs=in_specs,
        out_specs=pl.BlockSpec((tm, d), lambda i: (i, 0)),
        compiler_params=_cparams(("parallel",)),
        name="proj_residual_layernorm",
    )(*a_list, *w_parts, x, g.reshape(1, d), b.reshape(1, d))


RWKV_STEP_LANES = 4 * LANES
GDN_STEP_LANES = 4 * LANES
GDN_BLOCK_LANES = 2 * LANES


def _group_sum(x, gmat):
    return _dotx(x, gmat, 2, 1)


def _rwkv_kernel(pr_ref, pk_ref, pv_ref, pl_ref, mur_ref, muk_ref, muv_ref, mul_ref,
                 w0_ref, w2_ref, a0_ref, a2_ref, g2_ref, kk_ref, ka_ref, rk_ref, gng_ref, gnb_ref,
                 o_ref, prev_ref, prevl_ref, state_ref):
    c = pl.program_id(1)
    nb, ch, wd = pr_ref.shape

    @pl.when(c == 0)
    def _():
        prev_ref[...] = jnp.zeros_like(prev_ref)
        prevl_ref[...] = jnp.zeros_like(prevl_ref)
        state_ref[...] = jnp.zeros_like(state_ref)

    col = lax.broadcasted_iota(jnp.int32, (ch, LANES), 1)
    mlo = (col < RWKV_HEAD).astype(F32)
    mhi = 1.0 - mlo
    r2 = lax.broadcasted_iota(jnp.int32, (2 * ch, 2 * ch), 0)
    c2 = lax.broadcasted_iota(jnp.int32, (2 * ch, 2 * ch), 1)
    same = (r2 // ch) == (c2 // ch)
    strict = same & ((r2 % ch) > (c2 % ch))
    incl = same & ((r2 % ch) >= (c2 % ch))
    eye = (r2 == c2).astype(F32)
    rg = lax.broadcasted_iota(jnp.int32, (wd, wd), 0)
    cg = lax.broadcasted_iota(jnp.int32, (wd, wd), 1)
    gmat = ((rg // RWKV_HEAD) == (cg // RWKV_HEAD)).astype(BF16)
    rr = lax.broadcasted_iota(jnp.int32, (ch, ch), 0)
    cc = lax.broadcasted_iota(jnp.int32, (ch, ch), 1)
    tri = (rr >= cc).astype(BF16)

    def shifted(x, prev):
        return jnp.where(lax.broadcasted_iota(jnp.int32, x.shape, 0) == 0, prev, pltpu.roll(x, 1, 0))

    def stack(x):
        return jnp.concatenate([x * mlo, x * mhi], axis=0)

    pre = []
    for b in range(nb):
        raw = [pr_ref[b], pk_ref[b], pv_ref[b]]
        rawl = pl_ref[b]
        mus = [mur_ref[...], muk_ref[...], muv_ref[...]]
        mixed = []
        for i in range(3):
            xs = shifted(raw[i], prev_ref[b, i])
            mixed.append(raw[i] + (xs - raw[i]) * mus[i])
        xsl = shifted(rawl, prevl_ref[b])
        lora = rawl + (xsl - rawl) * mul_ref[...]
        for i in range(3):
            prev_ref[b, i] = raw[i][ch - 1:ch]
        prevl_ref[b] = rawl[ch - 1:ch]
        pre.append((mixed, lora))

    elem = []
    for b in range(nb):
        (r, k, v), lora = pre[b]
        l1, l2 = lora[:, :LANES], lora[:, LANES:]
        w = w0_ref[...] + _bdot(jnp.tanh(l1), w2_ref[...])
        ld = -math.exp(-0.5) * _sigmoid(w)
        a = _sigmoid(a0_ref[...] + _bdot(l1, a2_ref[...]))
        g = _bdot(_sigmoid(l2), g2_ref[...])
        kraw = k * kk_ref[...]
        kk = kraw * lax.rsqrt(_group_sum(kraw * kraw, gmat) + 1e-6)
        kmod = k * (1.0 + (a - 1.0) * ka_ref[...])
        cum = _dotx(tri, ld, 1, 3)
        w_t = jnp.exp(cum)
        w_prev = jnp.exp(cum - ld)
        w_inv = jnp.exp(-cum)
        w_end = jnp.exp(cum[ch - 1:ch])
        beta = kk * a
        fb = beta * w_inv
        fk = kmod * w_inv
        elem.append(dict(r=r, v=v, g=g, kmod=kmod, w_end=w_end, fa=-kk * w_prev, fb=fb, fk=fk, fq=r * w_t,
                         fbh=fb * w_end, fkh=fk * w_end))

    chains = [(b, hp) for b in range(nb) for hp in range(wd // LANES)]
    lane = lambda hp: slice(hp * LANES, (hp + 1) * LANES)
    take = lambda name: [stack(elem[b][name][:, lane(hp)]) for b, hp in chains]
    a2, b2, k2, q2, v2, bh2, kh2 = (take(n) for n in ("fa", "fb", "fk", "fq", "v", "fbh", "fkh"))
    aq = [jnp.concatenate([x, y], axis=0) for x, y in zip(a2, q2)]
    bk = [jnp.concatenate([x, y], axis=0) for x, y in zip(b2, k2)]
    sc = [_bdot(x, y, dims=NT) for x, y in zip(aq, bk)]
    n2 = 2 * ch
    lab = [jnp.where(strict, x[:n2, :n2], 0.0) for x in sc]
    lak = [jnp.where(strict, x[:n2, n2:], 0.0) for x in sc]
    grb = [jnp.where(incl, x[n2:, :n2], 0.0) for x in sc]
    grk = [jnp.where(incl, x[n2:, n2:], 0.0) for x in sc]
    minv = [eye + x for x in lab]
    pw = lab
    for _ in range(5):
        pw = [_bdot(x, x) for x in pw]
        minv = [m + _bdot(m, x) for m, x in zip(minv, pw)]
    t0 = [state_ref[b, hp] for b, hp in chains]
    at = [_bdot(x, t) for x, t in zip(aq, t0)]
    rhs = [x[:n2] + _bdot(l, vv) for x, l, vv in zip(at, lak, v2)]
    u2 = [_bdot(m, x) for m, x in zip(minv, rhs)]
    uv = [jnp.concatenate([u, vv], axis=0) for u, vv in zip(u2, v2)]
    lhs = [jnp.concatenate([jnp.concatenate([gb, gk], axis=1), jnp.concatenate([bh.T, kh.T], axis=1)], axis=0)
           for gb, gk, bh, kh in zip(grb, grk, bh2, kh2)]
    res = [_bdot(x, y) for x, y in zip(lhs, uv)]
    y2 = [x[n2:] + r[:n2] for x, r in zip(at, res)]
    for ci, (b, hp) in enumerate(chains):
        state_ref[b, hp] = elem[b]["w_end"][:, lane(hp)].T * t0[ci] + res[ci][n2:]

    for b in range(nb):
        e = elem[b]
        y = jnp.concatenate([y2[ci][:ch] + y2[ci][ch:] for ci, (bb, hp) in enumerate(chains) if bb == b], axis=1)
        inv_n = 1.0 / RWKV_HEAD
        mean = _group_sum(y, gmat) * inv_n
        yc = y - mean
        var = _group_sum(yc * yc, gmat) * inv_n
        yn = yc * lax.rsqrt(var + RWKV_GN_EPS) * gng_ref[...] + gnb_ref[...]
        bonus = _group_sum(e["r"] * e["kmod"] * rk_ref[...], gmat) * e["v"]
        o_ref[b] = (yn + bonus) * e["g"]


def rwkv_mix(p3, mu, w0, w2, a0, a2, g2, k_k, k_a, r_k, gn_g, gn_b):
    nb, s, _ = p3.shape
    width = w0.shape[0]
    wd = RWKV_STEP_LANES
    nstep = width // wd
    rank_w, rank_a = w2.shape[0], a2.shape[0]
    assert rank_w + rank_a == LANES and g2.shape[0] == LANES and 2 * CHUNK == LANES and 2 * RWKV_HEAD == LANES
    assert width % wd == 0 and (3 * width) % (2 * LANES) == 0
    w2p = jnp.concatenate([w2, jnp.zeros((rank_a, width), F32)], 0)
    a2p = jnp.concatenate([jnp.zeros((rank_w, width), F32), a2], 0)
    row = lambda t: t.reshape(1, -1)
    blk = lambda off: pl.BlockSpec((nb, CHUNK, wd), lambda j, c: (0, c, off + j))
    vec = lambda off: pl.BlockSpec((1, wd), lambda j, c: (0, off + j))
    mat = pl.BlockSpec((LANES, wd), lambda j, c: (0, j))
    lora_blk = 3 * width // (2 * LANES)
    mu2 = row(mu)
    return pl.pallas_call(
        _rwkv_kernel,
        out_shape=jax.ShapeDtypeStruct((nb, s, width), F32),
        grid=(nstep, s // CHUNK),
        in_specs=[blk(0), blk(nstep), blk(2 * nstep),
                  pl.BlockSpec((nb, CHUNK, 2 * LANES), lambda j, c: (0, c, lora_blk)),
                  vec(0), vec(nstep), vec(2 * nstep),
                  pl.BlockSpec((1, 2 * LANES), lambda j, c: (0, lora_blk)),
                  vec(0), mat, vec(0), mat, mat, vec(0), vec(0), vec(0), vec(0), vec(0)],
        out_specs=pl.BlockSpec((nb, CHUNK, wd), lambda j, c: (0, c, j)),
        scratch_shapes=[pltpu.VMEM((nb, 3, 1, wd), F32),
                        pltpu.VMEM((nb, 1, 2 * LANES), F32),
                        pltpu.VMEM((nb, wd // LANES, LANES, LANES), F32)],
        compiler_params=_cparams(("parallel", "arbitrary")),
        name="rwkv7_mixer",
    )(p3, p3, p3, p3, mu2, mu2, mu2, mu2, row(w0), w2p, row(a0), a2p, g2,
      row(k_k), row(k_a), row(r_k), row(gn_g), row(gn_b))


def _gdn_kernel(nsub, *refs):
    pq_refs, pk_refs, pv_refs, pg_refs = (refs[i * nsub:(i + 1) * nsub] for i in range(4))
    pbd_ref, cq_ref, ck_ref, cv_ref, alog_ref, dtb_ref, ng_ref, o_ref, halo_ref, state_ref = refs[4 * nsub:]
    j = pl.program_id(0)
    c = pl.program_id(1)
    nb, ch, sub = pq_refs[0].shape
    wd = nsub * sub
    nheads = alog_ref.shape[1]
    hps = wd // GDN_HEAD

    @pl.when(c == 0)
    def _():
        halo_ref[...] = jnp.zeros_like(halo_ref)
        state_ref[...] = jnp.zeros_like(state_ref)

    col = lax.broadcasted_iota(jnp.int32, (ch, LANES), 1)
    rr = lax.broadcasted_iota(jnp.int32, (ch, ch), 0)
    cc = lax.broadcasted_iota(jnp.int32, (ch, ch), 1)
    causal = rr >= cc
    strict = rr > cc
    tri = causal.astype(BF16)
    triu = (rr <= cc).astype(BF16)
    eye = (rr == cc).astype(F32)
    ones = jnp.ones((LANES, LANES), BF16)
    hcol = lax.broadcasted_iota(jnp.int32, (1, nheads), 1)

    def conv_silu(x_refs, w_ref, b, i):
        for si, x_ref in enumerate(x_refs):
            halo_ref[b, i, 8:, si * sub:(si + 1) * sub] = x_ref[b]
        acc = None
        for t in range(GDN_CONV):
            lo = 8 - (GDN_CONV - 1) + t
            term = halo_ref[b, i, lo:lo + ch, :] * w_ref[t:t + 1, :]
            acc = term if acc is None else acc + term
        halo_ref[b, i, 0:8, :] = halo_ref[b, i, ch:ch + 8, :]
        return _silu(acc)

    conv = [(conv_silu(pq_refs, cq_ref, b, 0), conv_silu(pk_refs, ck_ref, b, 1), conv_silu(pv_refs, cv_ref, b, 2))
            for b in range(nb)]
    chains = [(b, hh) for b in range(nb) for hh in range(hps)]
    lane = lambda hh: slice(hh * GDN_HEAD, (hh + 1) * GDN_HEAD)
    q = [conv[b][0][:, lane(hh)] for b, hh in chains]
    k = [conv[b][1][:, lane(hh)] for b, hh in chains]
    v = [conv[b][2][:, lane(hh)] for b, hh in chains]
    q = [x * lax.rsqrt(_dotx(x * x, ones, 2, 1) + 1e-6) * (GDN_HEAD ** -0.5) for x in q]
    k = [x * lax.rsqrt(_dotx(x * x, ones, 2, 1) + 1e-6) for x in k]
    beta, la = [], []
    for b, hh in chains:
        h = j * hps + hh
        bd = pbd_ref[b]
        a_coef = -jnp.exp(jnp.sum(jnp.where(hcol == h, alog_ref[...], 0.0), axis=-1, keepdims=True))
        dt_b = jnp.sum(jnp.where(hcol == h, dtb_ref[...], 0.0), axis=-1, keepdims=True)
        bcol = jnp.sum(jnp.where(col == h, bd, 0.0), axis=-1, keepdims=True)
        dcol = jnp.sum(jnp.where(col == h + nheads, bd, 0.0), axis=-1, keepdims=True)
        beta.append(_sigmoid(bcol))
        z = dcol + dt_b
        softplus = jnp.maximum(z, 0.0) + jnp.log(1.0 + jnp.exp(-jnp.abs(z)))
        la.append(jnp.broadcast_to(a_coef * softplus, (ch, LANES)))
    gc = [_dotx(tri, x, 1, 3) for x in la]
    gc_row = [_dotx(jnp.broadcast_to(x.T[0:1, :], (ch, ch)), triu, 3, 1) for x in la]
    decay = [jnp.where(causal, jnp.exp(jnp.where(causal, g[:, :ch] - gr, 0.0)), 0.0) for g, gr in zip(gc, gc_row)]
    eg = [jnp.exp(g) for g in gc]
    kb = [x * bt for x, bt in zip(k, beta)]
    lower = [jnp.where(strict, _bdot(x, y, dims=NT) * dc, 0.0) for x, y, dc in zip(kb, k, decay)]
    tinv = [eye - x for x in lower]
    pw = [-x for x in lower]
    for _ in range(5):
        pw = [_bdot(x, x) for x in pw]
        tinv = [t + _bdot(t, x) for t, x in zip(tinv, pw)]
    u = [_bdot(t, x * bt) for t, x, bt in zip(tinv, v, beta)]
    w = [_bdot(t, x * e) for t, x, e in zip(tinv, kb, eg)]
    qk = [_bdot(x, y, dims=NT) * dc for x, y, dc in zip(q, k, decay)]
    k_dec = [x * jnp.exp(g[ch - 1:ch, :] - g) for x, g in zip(k, gc)]
    t0 = [state_ref[b, hh] for b, hh in chains]
    v_new = [x - _bdot(y, t) for x, y, t in zip(u, w, t0)]
    out = [_bdot(x * e, t) + _bdot(y, vn) for x, e, t, y, vn in zip(q, eg, t0, qk, v_new)]
    for ci, (b, hh) in enumerate(chains):
        state_ref[b, hh] = t0[ci] * eg[ci][ch - 1:ch, 0:1] + _bdot(k_dec[ci].T, v_new[ci])
    for ci, (b, hh) in enumerate(chains):
        ms = _dotx(out[ci] * out[ci], ones, 2, 1) * (1.0 / GDN_HEAD)
        o = out[ci] * lax.rsqrt(ms + 1e-6) * ng_ref[...]
        si, lo = divmod(hh * GDN_HEAD, sub)
        o_ref[b, :, lane(hh)] = o * _silu(pg_refs[si][b, :, lo:lo + GDN_HEAD])


def gdn_mix(p3, col0, conv_w, a_log, dt_bias, norm_g):
    nb, s, _ = p3.shape
    nheads = a_log.shape[0]
    width = nheads * GDN_HEAD
    wd, sub = GDN_STEP_LANES, GDN_BLOCK_LANES
    nstep, nsub = width // wd, wd // sub
    assert col0 % sub == 0 and width % wd == 0 and wd % sub == 0 and GDN_HEAD == LANES

    def blks(col):
        base = col // sub
        return [pl.BlockSpec((nb, CHUNK, sub), lambda j, c, o=base + si: (0, c, o + j * nsub)) for si in range(nsub)]

    cw = pl.BlockSpec((GDN_CONV, wd), lambda j, c: (0, j))
    small = lambda n: pl.BlockSpec((1, n), lambda j, c: (0, 0))
    return pl.pallas_call(
        functools.partial(_gdn_kernel, nsub),
        out_shape=jax.ShapeDtypeStruct((nb, s, width), F32),
        grid=(nstep, s // CHUNK),
        in_specs=blks(col0) + blks(col0 + width) + blks(col0 + 2 * width) + blks(col0 + 3 * width) + [
            pl.BlockSpec((nb, CHUNK, LANES), lambda j, c: (0, c, (col0 + 4 * width) // LANES)),
            cw, cw, cw, small(nheads), small(nheads), small(LANES)],
        out_specs=pl.BlockSpec((nb, CHUNK, wd), lambda j, c: (0, c, j)),
        scratch_shapes=[pltpu.VMEM((nb, 3, 8 + CHUNK, wd), F32),
                        pltpu.VMEM((nb, wd // GDN_HEAD, LANES, LANES), F32)],
        compiler_params=_cparams(("parallel", "arbitrary")),
        name="gated_deltanet_mixer",
    )(*([p3] * (4 * nsub + 1)), conv_w[:, :width], conv_w[:, width:2 * width], conv_w[:, 2 * width:],
      a_log.reshape(1, -1), dt_bias.reshape(1, -1), norm_g.reshape(1, -1))


HALO_U = 32
HALO_Z = 16
CONV_TILE_ROWS = 128


def _odd_kernel(p_ref, cvw_ref, cvb_ref, lng_ref, lnb_ref, plw_ref, pls_ref, o_ref, ubuf_ref, zbuf_ref, conv_ref):
    s_idx = pl.program_id(1)
    ts = p_ref.shape[1]
    cw = cvb_ref.shape[1]
    pool_w = pls_ref.shape[1]
    pg = pool_w // len(POOL_WINDOWS)

    @pl.when(s_idx == 0)
    def _():
        ubuf_ref[0:HALO_U, :] = jnp.zeros((HALO_U, cw), F32)
        zbuf_ref[0:HALO_Z, :] = jnp.zeros((HALO_Z, pool_w), F32)

    pa = p_ref[0, :, 0:cw]
    pb = p_ref[0, :, cw:2 * cw]
    ubuf_ref[HALO_U:, :] = pa * _sigmoid(pb)
    base = HALO_U - (CONV_KERNEL - 1)
    for r0 in range(0, ts, CONV_TILE_ROWS):
        for c0 in range(0, cw, LANES):
            acc = jnp.broadcast_to(cvb_ref[:, c0:c0 + LANES], (CONV_TILE_ROWS, LANES))
            for j in range(CONV_KERNEL):
                lo = base + j + r0
                acc = acc + ubuf_ref[lo:lo + CONV_TILE_ROWS, c0:c0 + LANES] * cvw_ref[j:j + 1, c0:c0 + LANES]
            conv_ref[r0:r0 + CONV_TILE_ROWS, c0:c0 + LANES] = acc
    ubuf_ref[0:HALO_U, :] = ubuf_ref[ts:ts + HALO_U, :]
    o_ref[0, :, 0:cw] = _silu(_layer_norm(conv_ref[...], lng_ref[...], lnb_ref[...]))

    z = p_ref[0, :, 2 * cw:]
    zbuf_ref[HALO_Z:, :] = z
    t1 = (s_idx * ts + 1 + lax.broadcasted_iota(jnp.int32, (ts, 1), 0)).astype(F32)
    for gi, win in enumerate(POOL_WINDOWS):
        lo = gi * pg
        ssum = None
        for j in range(win):
            t = zbuf_ref[HALO_Z - j:HALO_Z - j + ts, lo:lo + pg]
            ssum = t if ssum is None else ssum + t
        pooled = ssum / jnp.minimum(t1, float(win)) - z[:, lo:lo + pg]
        mixed = _bdot(pooled, plw_ref[gi]) * pls_ref[:, lo:lo + pg]
        o_ref[0, :, cw + lo:cw + lo + pg] = mixed
    zbuf_ref[0:HALO_Z, :] = zbuf_ref[ts:ts + HALO_Z, :]


def odd_mix(p3, cv_w, cv_b, cv_ln_g, cv_ln_b, pl_w, pl_scale, ts=256):
    nb, s, cols = p3.shape
    cw = cv_b.shape[0]
    pool_w = pl_scale.shape[0]
    ts = min(ts, s)
    assert ts % CONV_TILE_ROWS == 0
    row = lambda t: t.reshape(1, -1)
    cvw = jnp.concatenate([cv_w, jnp.zeros((HALO_U - CONV_KERNEL, cw), F32)], 0)
    return pl.pallas_call(
        _odd_kernel,
        out_shape=jax.ShapeDtypeStruct((nb, s, cw + pool_w), F32),
        grid=(nb, s // ts),
        in_specs=[pl.BlockSpec((1, ts, cols), lambda b, i: (b, i, 0)),
                  _const_spec((HALO_U, cw)), _const_spec((1, cw)), _const_spec((1, cw)), _const_spec((1, cw)),
                  _const_spec(pl_w.shape), _const_spec((1, pool_w))],
        out_specs=pl.BlockSpec((1, ts, cw + pool_w), lambda b, i: (b, i, 0)),
        scratch_shapes=[pltpu.VMEM((HALO_U + ts, cw), F32), pltpu.VMEM((HALO_Z + ts, pool_w), F32),
                        pltpu.VMEM((ts, cw), F32)],
        compiler_params=_cparams(("parallel", "arbitrary")),
        name="conformer_pool_mixer",
    )(p3, cvw, row(cv_b), row(cv_ln_g), row(cv_ln_b), pl_w.astype(BF16), row(pl_scale))


def _xattn_kernel(x_ref, k_ref, v_ref, wq_ref, wo_ref, g_ref, b_ref, o_ref):
    x = x_ref[0]
    d = x.shape[1]
    dh = d // XA_HEADS
    q = _bdot(x, wq_ref[...])
    heads = []
    for hd in range(XA_HEADS):
        sl = slice(hd * dh, (hd + 1) * dh)
        s = _bdot(q[:, sl], k_ref[0, :, sl], NT) * (dh ** -0.5)
        s = s - jnp.max(s, axis=-1, keepdims=True)
        e = jnp.exp(s)
        prob = e / jnp.sum(e, axis=-1, keepdims=True)
        heads.append(_bdot(prob, v_ref[0, :, sl]))
    a = _bdot(jnp.concatenate(heads, axis=-1), wo_ref[...])
    o_ref[0] = _layer_norm(DEEPNORM_ALPHA * x + a, g_ref[...], b_ref[...])


def cross_attention_ln(x3, k3, v3, wq, wo, g, b, ts=256):
    nb, s, d = x3.shape
    m = k3.shape[1]
    ts = min(ts, s)
    return pl.pallas_call(
        _xattn_kernel,
        out_shape=jax.ShapeDtypeStruct((nb, s, d), F32),
        grid=(nb, s // ts),
        in_specs=[pl.BlockSpec((1, ts, d), lambda bi, i: (bi, i, 0)),
                  pl.BlockSpec((1, m, d), lambda bi, i: (bi, 0, 0)),
                  pl.BlockSpec((1, m, d), lambda bi, i: (bi, 0, 0)),
                  _const_spec((d, d)), _const_spec((d, d)), _const_spec((1, d)), _const_spec((1, d))],
        out_specs=pl.BlockSpec((1, ts, d), lambda bi, i: (bi, i, 0)),
        compiler_params=_cparams(("parallel", "parallel")),
        name="cross_attention_layernorm",
    )(x3, k3, v3, wq, wo, g.reshape(1, d), b.reshape(1, d))


def _router_kernel(x_ref, wr_ref, br_ref, idx_ref, gate_ref, rank_ref, cnt_ref, base_ref, tri_ref):
    i = pl.program_id(0)
    tm = x_ref.shape[0]
    ne = wr_ref.shape[0]

    @pl.when(i == 0)
    def _():
        base_ref[...] = jnp.zeros_like(base_ref)
        rr = lax.broadcasted_iota(jnp.int32, (tm, tm), 0)
        cc = lax.broadcasted_iota(jnp.int32, (tm, tm), 1)
        tri_ref[...] = (rr <= cc).astype(BF16)

    logits = _dotx(wr_ref[...], x_ref[...], dims=NT) + br_ref[...]
    eidx = lax.broadcasted_iota(jnp.int32, (ne, tm), 0)
    work = logits
    tops, sels = [], []
    for kk in range(TOP_K):
        mx = jnp.max(work, axis=0, keepdims=True)
        sel_idx = jnp.min(jnp.where(work == mx, eidx, ne), axis=0, keepdims=True)
        sel = eidx == sel_idx
        tops.append(mx)
        sels.append(sel)
        idx_ref[kk:kk + 1, :] = sel_idx
        work = jnp.where(sel, -jnp.inf, work)
    es = [jnp.exp(t - tops[0]) for t in tops]
    den = es[0] + es[1] + es[2] + es[3]
    for kk in range(TOP_K):
        gate_ref[kk:kk + 1, :] = es[kk] / den
    onehot = sels[0] | sels[1] | sels[2] | sels[3]
    oh = jnp.where(onehot, 1.0, 0.0)
    incl = jnp.dot(oh.astype(BF16), tri_ref[...], preferred_element_type=F32)
    before = base_ref[:, 0:1] + incl - oh
    for kk in range(TOP_K):
        rank = jnp.sum(jnp.where(sels[kk], before, 0.0), axis=0, keepdims=True)
        rank_ref[kk:kk + 1, :] = rank.astype(jnp.int32)
    total = base_ref[:, 0:1] + incl[:, tm - 1:tm]
    base_ref[...] = jnp.broadcast_to(total, base_ref.shape)
    cnt_ref[...] = jnp.broadcast_to(total, cnt_ref.shape).astype(jnp.int32)


def moe_route(xf, w_r, b_r, tm=512):
    n, d = xf.shape
    ne = w_r.shape[1]
    tm = min(tm, n)
    slot = lambda dt: jax.ShapeDtypeStruct((TOP_K, n), dt)
    return pl.pallas_call(
        _router_kernel,
        out_shape=(slot(jnp.int32), slot(F32), slot(jnp.int32), jax.ShapeDtypeStruct((ne, LANES), jnp.int32)),
        grid=(n // tm,),
        in_specs=[pl.BlockSpec((tm, d), lambda i: (i, 0)), _const_spec((ne, d)), _const_spec((ne, 1))],
        out_specs=(pl.BlockSpec((TOP_K, tm), lambda i: (0, i)), pl.BlockSpec((TOP_K, tm), lambda i: (0, i)),
                   pl.BlockSpec((TOP_K, tm), lambda i: (0, i)), pl.BlockSpec((ne, LANES), lambda i: (0, 0))),
        scratch_shapes=[pltpu.VMEM((ne, LANES), F32), pltpu.VMEM((tm, tm), BF16)],
        compiler_params=_cparams(("arbitrary",)),
        name="moe_router",
    )(xf, w_r.T, b_r.reshape(ne, 1))


def _dispatch_kernel(dest_ref, npad_ref, pstart_ref, nused_ref, x_ref, zero_ref, xs_ref, sem, zsem):
    i = pl.program_id(0)
    tm = x_ref.shape[0]
    ne = npad_ref.shape[0]
    nblk = xs_ref.shape[0] // MOE_ROWS

    @pl.when(i == 0)
    def _():
        def per_expert(e, carry):
            def one(r, c2):
                pltpu.make_async_copy(zero_ref.at[pl.ds(0, 1)], xs_ref.at[pl.ds(pstart_ref[e] + r, 1)], zsem).start()
                return c2
            lax.fori_loop(0, npad_ref[e], one, 0)

            def one_wait(r, c2):
                pltpu.make_async_copy(zero_ref.at[pl.ds(0, 1)], xs_ref.at[pl.ds(0, 1)], zsem).wait()
                return c2
            lax.fori_loop(0, npad_ref[e], one_wait, 0)
            return carry
        lax.fori_loop(0, ne, per_expert, 0)

        def unused(blk, carry):
            cp = pltpu.make_async_copy(zero_ref, xs_ref.at[pl.ds(blk * MOE_ROWS, MOE_ROWS)], zsem)
            cp.start()
            cp.wait()
            return carry
        lax.fori_loop(nused_ref[0], nblk, unused, 0)

    def issue(t, carry):
        for kk in range(TOP_K):
            pltpu.make_async_copy(x_ref.at[pl.ds(t, 1)], xs_ref.at[pl.ds(dest_ref[kk, t], 1)], sem).start(priority=kk % 2)
        return carry
    lax.fori_loop(0, tm, issue, 0)

    def drain(t, carry):
        for kk in range(TOP_K):
            pltpu.make_async_copy(x_ref.at[pl.ds(0, 1)], xs_ref.at[pl.ds(0, 1)], sem).wait()
        return carry
    lax.fori_loop(0, tm, drain, 0)


def moe_dispatch(xf, dest, npad, pad_fill_start, nused, cap, tm=256):
    n, d = xf.shape
    tm = min(tm, n)
    smem = pl.BlockSpec(memory_space=pltpu.SMEM)
    return pl.pallas_call(
        _dispatch_kernel,
        out_shape=jax.ShapeDtypeStruct((cap, d), F32),
        grid=(n // tm,),
        in_specs=[pl.BlockSpec((TOP_K, tm), lambda i: (0, i), memory_space=pltpu.SMEM), smem, smem, smem,
                  pl.BlockSpec((tm, d), lambda i: (i, 0)), _const_spec((MOE_ROWS, d))],
        out_specs=pl.BlockSpec(memory_space=pl.ANY),
        scratch_shapes=[pltpu.SemaphoreType.DMA, pltpu.SemaphoreType.DMA],
        compiler_params=_cparams(("arbitrary",)),
        name="moe_dispatch",
    )(dest, npad, pad_fill_start, nused, xf, jnp.zeros((MOE_ROWS, d), F32))


PAIR_TILE = 2 * LANES


def _expert_changed(be_ref, i):
    return (i == 0) | (be_ref[i] != be_ref[jnp.maximum(i - 1, 0)])


def _next_weights(be_ref, nxt_ref, i, layer, w_hbm, wbuf_ref, sem, use):
    copy = lambda e: pltpu.make_async_copy(w_hbm.at[layer, e], wbuf_ref, sem)
    e = be_ref[i]

    @pl.when(i == 0)
    def _():
        copy(e).start()

    copy(e).wait()
    use()
    nx = nxt_ref[e]

    @pl.when(nx >= 0)
    def _():
        copy(nx).start()


def _expert_kernel(layer, be_ref, nused_ref, nxt_ref, xs_ref, wgu_hbm, bgu_ref, wdn_hbm, bdn_ref, y_ref,
                   wgu_buf, wperm_ref, bperm_ref, wdn_buf, wd_ref, act_ref, sem_gu, sem_dn):
    i = pl.program_id(0)
    active = i < nused_ref[0]
    ntile = wgu_buf.shape[1] // PAIR_TILE

    def deinterleave():
        r = lax.broadcasted_iota(jnp.int32, (PAIR_TILE, PAIR_TILE), 0)
        c = lax.broadcasted_iota(jnp.int32, (PAIR_TILE, PAIR_TILE), 1)
        perm = (r == jnp.where(c < LANES, 2 * c, 2 * (c - LANES) + 1)).astype(BF16)
        for t in range(ntile):
            sl = slice(t * PAIR_TILE, (t + 1) * PAIR_TILE)
            w = wgu_buf[:, sl].astype(BF16)
            wperm_ref[:, sl] = jnp.dot(w, perm, preferred_element_type=F32).astype(BF16)
            bias = jnp.broadcast_to(bgu_ref[0, 0, :, sl], (8, PAIR_TILE))
            bperm_ref[:, sl] = _dotx(bias, perm, 3, 1)

    def cast():
        wd_ref[...] = wdn_buf[...].astype(BF16)

    @pl.when(active & _expert_changed(be_ref, i))
    def _():
        _next_weights(be_ref, nxt_ref, i, layer, wgu_hbm, wgu_buf, sem_gu, deinterleave)
        _next_weights(be_ref, nxt_ref, i, layer, wdn_hbm, wdn_buf, sem_dn, cast)

    @pl.when(active)
    def _():
        x = xs_ref[...].astype(BF16)
        for t in range(ntile):
            sl = slice(t * PAIR_TILE, (t + 1) * PAIR_TILE)
            h = jnp.dot(x, wperm_ref[:, sl], preferred_element_type=F32) + bperm_ref[0:1, sl]
            gate = jnp.minimum(h[:, :LANES], SWIGLU_LIMIT)
            up = jnp.clip(h[:, LANES:], -SWIGLU_LIMIT, SWIGLU_LIMIT)
            act = (up + 1.0) * gate * _sigmoid(SWIGLU_ALPHA * gate)
            act_ref[:, t * LANES:(t + 1) * LANES] = act.astype(BF16)
        y_ref[...] = jnp.dot(act_ref[...], wd_ref[...], preferred_element_type=F32) + bdn_ref[0, 0]

    @pl.when(jnp.logical_not(active))
    def _():
        y_ref[...] = jnp.zeros_like(y_ref)


def moe_experts(xs, block_expert, nused, next_expert, layer, w_gu, b_gu, w_dn, b_dn):
    cap, d = xs.shape
    nl, ne, _, d2 = w_gu.shape
    dff = d2 // 2
    nblk = cap // MOE_ROWS
    blk = lambda i, be, nu, nx: (i, 0)
    bias = lambda n: pl.BlockSpec((1, 1, 1, n), lambda i, be, nu, nx: (layer, be[i], 0, 0))
    hbm = pl.BlockSpec(memory_space=pl.ANY)
    return pl.pallas_call(
        functools.partial(_expert_kernel, layer),
        out_shape=jax.ShapeDtypeStruct((cap, d), F32),
        grid_spec=pltpu.PrefetchScalarGridSpec(
            num_scalar_prefetch=3, grid=(nblk,),
            in_specs=[pl.BlockSpec((MOE_ROWS, d), blk), hbm, bias(d2), hbm, bias(d)],
            out_specs=pl.BlockSpec((MOE_ROWS, d), blk),
            scratch_shapes=[pltpu.VMEM((d, d2), F32), pltpu.VMEM((d, d2), BF16), pltpu.VMEM((8, d2), F32),
                            pltpu.VMEM((dff, d), F32), pltpu.VMEM((dff, d), BF16), pltpu.VMEM((MOE_ROWS, dff), BF16),
                            pltpu.SemaphoreType.DMA, pltpu.SemaphoreType.DMA]),
        compiler_params=_cparams(("arbitrary",)),
        name="moe_experts",
    )(block_expert, nused, next_expert, xs, w_gu, b_gu.reshape(nl, ne, 1, d2), w_dn, b_dn.reshape(nl, ne, 1, d))


def _plan_kernel(idx_ref, rank_ref, cnt_ref, dest_ref):
    ne = cnt_ref.shape[0]
    tm = idx_ref.shape[1]
    padded = ((cnt_ref[...] + (MOE_ROWS - 1)) >> MOE_ROWS_LOG2) << MOE_ROWS_LOG2
    r = lax.broadcasted_iota(jnp.int32, (ne, ne), 0)
    c = lax.broadcasted_iota(jnp.int32, (ne, ne), 1)
    pad_start = _dotx((c < r).astype(BF16), padded.astype(F32), 1, 3)[:, 0:1]
    eidx = lax.broadcasted_iota(jnp.int32, (ne, tm), 0)
    for kk in range(TOP_K):
        start = jnp.sum(jnp.where(eidx == idx_ref[kk:kk + 1, :], pad_start, 0.0), axis=0, keepdims=True)
        dest_ref[kk:kk + 1, :] = rank_ref[kk:kk + 1, :] + start.astype(jnp.int32)


def moe_plan(idx, rank, cnt, tm=2048):
    k, n = idx.shape
    ne = cnt.shape[0]
    tm = min(tm, n)
    spec = pl.BlockSpec((k, tm), lambda i: (0, i))
    return pl.pallas_call(
        _plan_kernel,
        out_shape=jax.ShapeDtypeStruct((k, n), jnp.int32),
        grid=(n // tm,),
        in_specs=[spec, spec, _const_spec((ne, LANES))],
        out_specs=spec,
        compiler_params=_cparams(("parallel",)),
        name="moe_plan",
    )(idx, rank, cnt)


def _combine_kernel(dest_ref, y_ref, gate_ref, x_ref, g_ref, b_ref, o_ref, buf_ref, sem):
    tm = x_ref.shape[0]

    def issue(t, carry):
        for kk in range(TOP_K):
            pltpu.make_async_copy(y_ref.at[pl.ds(dest_ref[kk, t], 1)], buf_ref.at[kk, pl.ds(t, 1)], sem).start(
                priority=kk % 2)
        return carry
    lax.fori_loop(0, tm, issue, 0)

    def drain(t, carry):
        for kk in range(TOP_K):
            pltpu.make_async_copy(y_ref.at[pl.ds(0, 1)], buf_ref.at[kk, pl.ds(0, 1)], sem).wait()
        return carry
    lax.fori_loop(0, tm, drain, 0)

    f = None
    for kk in range(TOP_K):
        t = buf_ref[kk] * gate_ref[:, kk:kk + 1]
        f = t if f is None else f + t
    o_ref[...] = _layer_norm(DEEPNORM_ALPHA * x_ref[...] + f, g_ref[...], b_ref[...])


def moe_combine_ln(y, dest, gate_t, xf, g, b, tm=256):
    n, d = xf.shape
    tm = min(tm, n)
    return pl.pallas_call(
        _combine_kernel,
        out_shape=jax.ShapeDtypeStruct((n, d), F32),
        grid=(n // tm,),
        in_specs=[pl.BlockSpec((TOP_K, tm), lambda i: (0, i), memory_space=pltpu.SMEM),
                  pl.BlockSpec(memory_space=pl.ANY),
                  pl.BlockSpec((tm, TOP_K), lambda i: (i, 0)),
                  pl.BlockSpec((tm, d), lambda i: (i, 0)), _const_spec((1, d)), _const_spec((1, d))],
        out_specs=pl.BlockSpec((tm, d), lambda i: (i, 0)),
        scratch_shapes=[pltpu.VMEM((TOP_K, tm, d), F32), pltpu.SemaphoreType.DMA],
        compiler_params=_cparams(("arbitrary",)),
        name="moe_combine_layernorm",
    )(dest, y, gate_t, xf, g.reshape(1, d), b.reshape(1, d))


def moe_ffn_ln(xf, layer, w_r, b_r, w_gu, b_gu, w_dn, b_dn, g, b):
    n, d = xf.shape
    ne = w_r.shape[1]
    idx, gate, rank, cnt = moe_route(xf, w_r, b_r)
    dest = moe_plan(idx, rank, cnt)
    counts = cnt[:, 0]
    padded = -(-counts // MOE_ROWS) * MOE_ROWS
    pad_end = jnp.cumsum(padded)
    pad_start = pad_end - padded
    cap = n * TOP_K + ne * MOE_ROWS
    nblk = cap // MOE_ROWS
    nused = (pad_end[-1] // MOE_ROWS).astype(jnp.int32).reshape(1)
    blk_row = jnp.arange(nblk, dtype=jnp.int32) * MOE_ROWS
    block_expert = jnp.minimum(jnp.sum(blk_row[:, None] >= pad_end[None, :], axis=1), ne - 1).astype(jnp.int32)
    xs = moe_dispatch(xf, dest, (padded - counts).astype(jnp.int32), (pad_start + counts).astype(jnp.int32),
                      nused, cap)
    eidx = jnp.arange(ne, dtype=jnp.int32)
    later = (eidx[None, :] > eidx[:, None]) & (counts[None, :] > 0)
    next_expert = jnp.min(jnp.where(later, eidx[None, :], ne), axis=1)
    next_expert = jnp.where(next_expert < ne, next_expert, -1).astype(jnp.int32)
    y = moe_experts(xs, block_expert, nused, next_expert, layer, w_gu, b_gu, w_dn, b_dn)
    return moe_combine_ln(y, dest, gate.T, xf, g, b)


def kernel(x, mem, ev_w_in, ev_mu, rk_w0, rk_w2, rk_a0, rk_a2, rk_g2, rk_kk, rk_ka, rk_rk, rk_gn_g, rk_gn_b,
           gd_conv, gd_a_log, gd_dt_bias, gd_norm_g, ev_w_out, od_w_in, cv_w, cv_b, cv_ln_g, cv_ln_b, pl_w,
           pl_scale, od_w_out, xa_wq, xa_wk, xa_wv, xa_wo, moe_wr, moe_br, moe_wgu, moe_bgu, moe_wdn, moe_bdn,
           ln_g, ln_b):
    nb, s, d = x.shape
    n = nb * s
    m = mem.shape[1]
    xf = x.reshape(n, d)
    memf = mem.reshape(nb * m, d)
    for layer in range(DEPTH):
        i = layer // 2
        if layer % 2 == 0:
            rwkv_cols = ev_mu.shape[1]
            p3 = matmul(xf, ev_w_in, i).reshape(nb, s, -1)
            ya = rwkv_mix(p3, ev_mu[i], rk_w0[i], rk_w2[i], rk_a0[i], rk_a2[i], rk_g2[i], rk_kk[i], rk_ka[i],
                          rk_rk[i].reshape(-1), rk_gn_g[i], rk_gn_b[i])
            yb = gdn_mix(p3, rwkv_cols, gd_conv[i], gd_a_log[i], gd_dt_bias[i], gd_norm_g[i])
            mixed = [ya.reshape(n, -1), yb.reshape(n, -1)]
            w_out = ev_w_out[i]
        else:
            p3 = matmul(xf, od_w_in, i).reshape(nb, s, -1)
            mixed = [odd_mix(p3, cv_w[i], cv_b[i], cv_ln_g[i], cv_ln_b[i], pl_w[i], pl_scale[i]).reshape(n, -1)]
            w_out = od_w_out[i]
        xf = mm_res_ln(mixed, w_out.astype(BF16), xf, ln_g[layer, 0], ln_b[layer, 0])
        k3 = matmul(memf, xa_wk, layer).reshape(nb, m, d)
        v3 = matmul(memf, xa_wv, layer).reshape(nb, m, d)
        xf = cross_attention_ln(xf.reshape(nb, s, d), k3, v3, xa_wq[layer].astype(BF16), xa_wo[layer].astype(BF16),
                                ln_g[layer, 1], ln_b[layer, 1]).reshape(n, d)
        xf = moe_ffn_ln(xf, layer, moe_wr[layer], moe_br[layer], moe_wgu, moe_bgu, moe_wdn, moe_bdn,
                        ln_g[layer, 2], ln_b[layer, 2])
    return xf.reshape(nb, s, d)
```

```python
import functools
import math

import jax
import jax.numpy as jnp
from jax import lax
from jax.experimental import pallas as pl
from jax.experimental.pallas import tpu as pltpu

F32 = jnp.float32
BF16 = jnp.bfloat16

LANES = 128
VMEM_LIMIT = 56 * 1024 * 1024

DEPTH = 2
DEEPNORM_ALPHA = (2 * DEPTH) ** 0.25
LN_EPS = 1e-5
CHUNK = 64
RWKV_HEAD = 64
RWKV_GN_EPS = 64e-5
GDN_HEAD = 128
GDN_CONV = 4
CONV_KERNEL = 31
POOL_WINDOWS = (2, 4, 8, 16)
XA_HEADS = 4
N_EXPERTS = 32
TOP_K = 4
SWIGLU_LIMIT = 7.0
SWIGLU_ALPHA = 1.702
IN_PROJ_ROWS = 2048
MOE_ROWS_LOG2 = 8
MOE_ROWS = 1 << MOE_ROWS_LOG2

NN = (((1,), (0,)), ((), ()))
NT = (((1,), (1,)), ((), ()))


def _cparams(sem):
    return pltpu.CompilerParams(dimension_semantics=sem, vmem_limit_bytes=VMEM_LIMIT)


def _bdot(a, b, dims=NN):
    return lax.dot_general(a.astype(BF16), b.astype(BF16), dims, preferred_element_type=F32)


def _parts(x, n):
    out, rem = [], x
    for i in range(n):
        h = rem.astype(BF16)
        out.append(h)
        if i + 1 < n:
            rem = rem - h.astype(F32)
    return out


def _dotx(a, b, na=2, nb=2, dims=NN):
    ap, bp = _parts(a, na), _parts(b, nb)
    acc = None
    for i, ai in enumerate(ap):
        for j, bj in enumerate(bp):
            if i + j < max(na, nb):
                t = lax.dot_general(ai, bj, dims, preferred_element_type=F32)
                acc = t if acc is None else acc + t
    return acc


def _layer_norm(v, g, b):
    mu = jnp.mean(v, axis=-1, keepdims=True)
    c = v - mu
    var = jnp.mean(c * c, axis=-1, keepdims=True)
    return c * lax.rsqrt(var + LN_EPS) * g + b


def _sigmoid(x):
    return jax.nn.sigmoid(x)


def _silu(x):
    return x * jax.nn.sigmoid(x)


def _const_spec(shape):
    return pl.BlockSpec(shape, lambda *_: (0,) * len(shape), pipeline_mode=pl.Buffered(1))


def _mm_kernel(x_ref, w_ref, o_ref):
    o_ref[...] = _bdot(x_ref[...], w_ref[0])


def matmul(x, w_stack, idx, tm=1024, tn=512):
    m, k = x.shape
    n = w_stack.shape[2]
    tm = min(tm, m)
    return pl.pallas_call(
        _mm_kernel,
        out_shape=jax.ShapeDtypeStruct((m, n), F32),
        grid=(pl.cdiv(m, tm), pl.cdiv(n, tn)),
        in_specs=[pl.BlockSpec((tm, k), lambda i, j: (i, 0)),
                  pl.BlockSpec((1, k, tn), lambda i, j: (idx, 0, j))],
        out_specs=pl.BlockSpec((tm, tn), lambda i, j: (i, j)),
        compiler_params=_cparams(("parallel", "arbitrary")),
        name="dense_matmul",
    )(x, w_stack)


def _mm_res_ln_kernel(n_in, *refs):
    a_refs = refs[:n_in]
    w_refs = refs[n_in:2 * n_in]
    x_ref, g_ref, b_ref, o_ref = refs[2 * n_in:]
    h = None
    for a_ref, w_ref in zip(a_refs, w_refs):
        t = _bdot(a_ref[...], w_ref[...])
        h = t if h is None else h + t
    o_ref[...] = _layer_norm(DEEPNORM_ALPHA * x_ref[...] + h, g_ref[...], b_ref[...])


def mm_res_ln(a_list, w, x, g, b, tm=256):
    m, d = x.shape
    tm = min(tm, m)
    in_specs, off = [], 0
    for a in a_list:
        in_specs.append(pl.BlockSpec((tm, a.shape[1]), lambda i: (i, 0)))
    w_parts = []
    for a in a_list:
        ka = a.shape[1]
        w_parts.append(lax.slice_in_dim(w, off, off + ka, axis=0))
        in_specs.append(_const_spec((ka, d)))
        off += ka
    in_specs += [pl.BlockSpec((tm, d), lambda i: (i, 0)), _const_spec((1, d)), _const_spec((1, d))]
    return pl.pallas_call(
        functools.partial(_mm_res_ln_kernel, len(a_list)),
        out_shape=jax.ShapeDtypeStruct((m, d), F32),
        grid=(m // tm,),
        in_specs=in_specs,
        out_specs=pl.BlockSpec((tm, d), lambda i: (i, 0)),
        compiler_params=_cparams(("parallel",)),
        name="proj_residual_layernorm",
    )(*a_list, *w_parts, x, g.reshape(1, d), b.reshape(1, d))


RWKV_STEP_LANES = 8 * LANES
GDN_STEP_LANES = 8 * LANES
GDN_BLOCK_LANES = 2 * LANES


def _group_sum(x, gmat):
    return _dotx(x, gmat, 2, 1)


def _rwkv_kernel(pr_ref, pk_ref, pv_ref, pl_ref, mur_ref, muk_ref, muv_ref, mul_ref,
                 w0_ref, w2_ref, a0_ref, a2_ref, g2_ref, kk_ref, ka_ref, rk_ref, gng_ref, gnb_ref,
                 o_ref, prev_ref, prevl_ref, state_ref):
    c = pl.program_id(1)
    nb, ch, wd = pr_ref.shape

    @pl.when(c == 0)
    def _():
        prev_ref[...] = jnp.zeros_like(prev_ref)
        prevl_ref[...] = jnp.zeros_like(prevl_ref)
        state_ref[...] = jnp.zeros_like(state_ref)

    col = lax.broadcasted_iota(jnp.int32, (ch, LANES), 1)
    mlo = (col < RWKV_HEAD).astype(F32)
    mhi = 1.0 - mlo
    r2 = lax.broadcasted_iota(jnp.int32, (2 * ch, 2 * ch), 0)
    c2 = lax.broadcasted_iota(jnp.int32, (2 * ch, 2 * ch), 1)
    same = (r2 // ch) == (c2 // ch)
    strict = same & ((r2 % ch) > (c2 % ch))
    incl = same & ((r2 % ch) >= (c2 % ch))
    eye = (r2 == c2).astype(F32)
    rg = lax.broadcasted_iota(jnp.int32, (wd, wd), 0)
    cg = lax.broadcasted_iota(jnp.int32, (wd, wd), 1)
    gmat = ((rg // RWKV_HEAD) == (cg // RWKV_HEAD)).astype(BF16)
    rr = lax.broadcasted_iota(jnp.int32, (ch, ch), 0)
    cc = lax.broadcasted_iota(jnp.int32, (ch, ch), 1)
    tri = (rr >= cc).astype(BF16)

    def shifted(x, prev):
        return jnp.where(lax.broadcasted_iota(jnp.int32, x.shape, 0) == 0, prev, pltpu.roll(x, 1, 0))

    def stack(x):
        return jnp.concatenate([x * mlo, x * mhi], axis=0)

    pre = []
    for b in range(nb):
        raw = [pr_ref[b], pk_ref[b], pv_ref[b]]
        rawl = pl_ref[b]
        mus = [mur_ref[...], muk_ref[...], muv_ref[...]]
        mixed = []
        for i in range(3):
            xs = shifted(raw[i], prev_ref[b, i])
            mixed.append(raw[i] + (xs - raw[i]) * mus[i])
        xsl = shifted(rawl, prevl_ref[b])
        lora = rawl + (xsl - rawl) * mul_ref[...]
        for i in range(3):
            prev_ref[b, i] = raw[i][ch - 1:ch]
        prevl_ref[b] = rawl[ch - 1:ch]
        pre.append((mixed, lora))

    elem = []
    for b in range(nb):
        (r, k, v), lora = pre[b]
        l1, l2 = lora[:, :LANES], lora[:, LANES:]
        w = w0_ref[...] + _bdot(jnp.tanh(l1), w2_ref[...])
        ld = -math.exp(-0.5) * _sigmoid(w)
        a = _sigmoid(a0_ref[...] + _bdot(l1, a2_ref[...]))
        g = _bdot(_sigmoid(l2), g2_ref[...])
        kraw = k * kk_ref[...]
        kk = kraw * lax.rsqrt(_group_sum(kraw * kraw, gmat) + 1e-6)
        kmod = k * (1.0 + (a - 1.0) * ka_ref[...])
        cum = _dotx(tri, ld, 1, 3)
        w_t = jnp.exp(cum)
        w_prev = jnp.exp(cum - ld)
        w_inv = jnp.exp(-cum)
        w_end = jnp.exp(cum[ch - 1:ch])
        beta = kk * a
        fb = beta * w_inv
        fk = kmod * w_inv
        elem.append(dict(r=r, v=v, g=g, kmod=kmod, w_end=w_end, fa=-kk * w_prev, fb=fb, fk=fk, fq=r * w_t,
                         fbh=fb * w_end, fkh=fk * w_end))

    chains = [(b, hp) for b in range(nb) for hp in range(wd // LANES)]
    lane = lambda hp: slice(hp * LANES, (hp + 1) * LANES)
    take = lambda name: [stack(elem[b][name][:, lane(hp)]) for b, hp in chains]
    a2, b2, k2, q2, v2, bh2, kh2 = (take(n) for n in ("fa", "fb", "fk", "fq", "v", "fbh", "fkh"))
    aq = [jnp.concatenate([x, y], axis=0) for x, y in zip(a2, q2)]
    bk = [jnp.concatenate([x, y], axis=0) for x, y in zip(b2, k2)]
    sc = [_bdot(x, y, dims=NT) for x, y in zip(aq, bk)]
    n2 = 2 * ch
    lab = [jnp.where(strict, x[:n2, :n2], 0.0) for x in sc]
    lak = [jnp.where(strict, x[:n2, n2:], 0.0) for x in sc]
    grb = [jnp.where(incl, x[n2:, :n2], 0.0) for x in sc]
    grk = [jnp.where(incl, x[n2:, n2:], 0.0) for x in sc]
    minv = [eye + x for x in lab]
    pw = lab
    for _ in range(5):
        pw = [_bdot(x, x) for x in pw]
        minv = [m + _bdot(m, x) for m, x in zip(minv, pw)]
    t0 = [state_ref[b, hp] for b, hp in chains]
    at = [_bdot(x, t) for x, t in zip(aq, t0)]
    rhs = [x[:n2] + _bdot(l, vv) for x, l, vv in zip(at, lak, v2)]
    u2 = [_bdot(m, x) for m, x in zip(minv, rhs)]
    uv = [jnp.concatenate([u, vv], axis=0) for u, vv in zip(u2, v2)]
    lhs = [jnp.concatenate([jnp.concatenate([gb, gk], axis=1), jnp.concatenate([bh.T, kh.T], axis=1)], axis=0)
           for gb, gk, bh, kh in zip(grb, grk, bh2, kh2)]
    res = [_bdot(x, y) for x, y in zip(lhs, uv)]
    y2 = [x[n2:] + r[:n2] for x, r in zip(at, res)]
    for ci, (b, hp) in enumerate(chains):
        state_ref[b, hp] = elem[b]["w_end"][:, lane(hp)].T * t0[ci] + res[ci][n2:]

    for b in range(nb):
        e = elem[b]
        y = jnp.concatenate([y2[ci][:ch] + y2[ci][ch:] for ci, (bb, hp) in enumerate(chains) if bb == b], axis=1)
        inv_n = 1.0 / RWKV_HEAD
        mean = _group_sum(y, gmat) * inv_n
        yc = y - mean
        var = _group_sum(yc * yc, gmat) * inv_n
        yn = yc * lax.rsqrt(var + RWKV_GN_EPS) * gng_ref[...] + gnb_ref[...]
        bonus = _group_sum(e["r"] * e["kmod"] * rk_ref[...], gmat) * e["v"]
        o_ref[b] = (yn + bonus) * e["g"]


def rwkv_mix(p3, mu, w0, w2, a0, a2, g2, k_k, k_a, r_k, gn_g, gn_b):
    nb, s, _ = p3.shape
    width = w0.shape[0]
    wd = RWKV_STEP_LANES
    nstep = width // wd
    rank_w, rank_a = w2.shape[0], a2.shape[0]
    assert rank_w + rank_a == LANES and g2.shape[0] == LANES and 2 * CHUNK == LANES and 2 * RWKV_HEAD == LANES
    assert width % wd == 0 and (3 * width) % (2 * LANES) == 0
    w2p = jnp.concatenate([w2, jnp.zeros((rank_a, width), F32)], 0)
    a2p = jnp.concatenate([jnp.zeros((rank_w, width), F32), a2], 0)
    row = lambda t: t.reshape(1, -1)
    blk = lambda off: pl.BlockSpec((nb, CHUNK, wd), lambda j, c: (0, c, off + j))
    vec = lambda off: pl.BlockSpec((1, wd), lambda j, c: (0, off + j))
    mat = pl.BlockSpec((LANES, wd), lambda j, c: (0, j))
    lora_blk = 3 * width // (2 * LANES)
    mu2 = row(mu)
    return pl.pallas_call(
        _rwkv_kernel,
        out_shape=jax.ShapeDtypeStruct((nb, s, width), F32),
        grid=(nstep, s // CHUNK),
        in_specs=[blk(0), blk(nstep), blk(2 * nstep),
                  pl.BlockSpec((nb, CHUNK, 2 * LANES), lambda j, c: (0, c, lora_blk)),
                  vec(0), vec(nstep), vec(2 * nstep),
                  pl.BlockSpec((1, 2 * LANES), lambda j, c: (0, lora_blk)),
                  vec(0), mat, vec(0), mat, mat, vec(0), vec(0), vec(0), vec(0), vec(0)],
        out_specs=pl.BlockSpec((nb, CHUNK, wd), lambda j, c: (0, c, j)),
        scratch_shapes=[pltpu.VMEM((nb, 3, 1, wd), F32),
                        pltpu.VMEM((nb, 1, 2 * LANES), F32),
                        pltpu.VMEM((nb, wd // LANES, LANES, LANES), F32)],
        compiler_params=_cparams(("parallel", "arbitrary")),
        name="rwkv7_mixer",
    )(p3, p3, p3, p3, mu2, mu2, mu2, mu2, row(w0), w2p, row(a0), a2p, g2,
      row(k_k), row(k_a), row(r_k), row(gn_g), row(gn_b))


def _gdn_kernel(nsub, *refs):
    pq_refs, pk_refs, pv_refs, pg_refs = (refs[i * nsub:(i + 1) * nsub] for i in range(4))
    pbd_ref, cq_ref, ck_ref, cv_ref, alog_ref, dtb_ref, ng_ref, o_ref, halo_ref, state_ref = refs[4 * nsub:]
    j = pl.program_id(0)
    c = pl.program_id(1)
    nb, ch, sub = pq_refs[0].shape
    wd = nsub * sub
    nheads = alog_ref.shape[1]
    hps = wd // GDN_HEAD

    @pl.when(c == 0)
    def _():
        halo_ref[...] = jnp.zeros_like(halo_ref)
        state_ref[...] = jnp.zeros_like(state_ref)

    col = lax.broadcasted_iota(jnp.int32, (ch, LANES), 1)
    rr = lax.broadcasted_iota(jnp.int32, (ch, ch), 0)
    cc = lax.broadcasted_iota(jnp.int32, (ch, ch), 1)
    causal = rr >= cc
    strict = rr > cc
    tri = causal.astype(BF16)
    triu = (rr <= cc).astype(BF16)
    eye = (rr == cc).astype(F32)
    ones = jnp.ones((LANES, LANES), BF16)
    hcol = lax.broadcasted_iota(jnp.int32, (1, nheads), 1)

    def conv_silu(x_refs, w_ref, b, i):
        for si, x_ref in enumerate(x_refs):
            halo_ref[b, i, 8:, si * sub:(si + 1) * sub] = x_ref[b]
        acc = None
        for t in range(GDN_CONV):
            lo = 8 - (GDN_CONV - 1) + t
            term = halo_ref[b, i, lo:lo + ch, :] * w_ref[t:t + 1, :]
            acc = term if acc is None else acc + term
        halo_ref[b, i, 0:8, :] = halo_ref[b, i, ch:ch + 8, :]
        return _silu(acc)

    conv = [(conv_silu(pq_refs, cq_ref, b, 0), conv_silu(pk_refs, ck_ref, b, 1), conv_silu(pv_refs, cv_ref, b, 2))
            for b in range(nb)]
    chains = [(b, hh) for b in range(nb) for hh in range(hps)]
    lane = lambda hh: slice(hh * GDN_HEAD, (hh + 1) * GDN_HEAD)
    q = [conv[b][0][:, lane(hh)] for b, hh in chains]
    k = [conv[b][1][:, lane(hh)] for b, hh in chains]
    v = [conv[b][2][:, lane(hh)] for b, hh in chains]
    q = [x * lax.rsqrt(_dotx(x * x, ones, 2, 1) + 1e-6) * (GDN_HEAD ** -0.5) for x in q]
    k = [x * lax.rsqrt(_dotx(x * x, ones, 2, 1) + 1e-6) for x in k]
    beta, la = [], []
    for b, hh in chains:
        h = j * hps + hh
        bd = pbd_ref[b]
        a_coef = -jnp.exp(jnp.sum(jnp.where(hcol == h, alog_ref[...], 0.0), axis=-1, keepdims=True))
        dt_b = jnp.sum(jnp.where(hcol == h, dtb_ref[...], 0.0), axis=-1, keepdims=True)
        bcol = jnp.sum(jnp.where(col == h, bd, 0.0), axis=-1, keepdims=True)
        dcol = jnp.sum(jnp.where(col == h + nheads, bd, 0.0), axis=-1, keepdims=True)
        beta.append(_sigmoid(bcol))
        z = dcol + dt_b
        softplus = jnp.maximum(z, 0.0) + jnp.log(1.0 + jnp.exp(-jnp.abs(z)))
        la.append(jnp.broadcast_to(a_coef * softplus, (ch, LANES)))
    gc = [_dotx(tri, x, 1, 3) for x in la]
    gc_row = [_dotx(jnp.broadcast_to(x.T[0:1, :], (ch, ch)), triu, 3, 1) for x in la]
    decay = [jnp.where(causal, jnp.exp(jnp.where(causal, g[:, :ch] - gr, 0.0)), 0.0) for g, gr in zip(gc, gc_row)]
    eg = [jnp.exp(g) for g in gc]
    kb = [x * bt for x, bt in zip(k, beta)]
    lower = [jnp.where(strict, _bdot(x, y, dims=NT) * dc, 0.0) for x, y, dc in zip(kb, k, decay)]
    tinv = [eye - x for x in lower]
    pw = [-x for x in lower]
    for _ in range(5):
        pw = [_bdot(x, x) for x in pw]
        tinv = [t + _bdot(t, x) for t, x in zip(tinv, pw)]
    u = [_bdot(t, x * bt) for t, x, bt in zip(tinv, v, beta)]
    w = [_bdot(t, x * e) for t, x, e in zip(tinv, kb, eg)]
    qk = [_bdot(x, y, dims=NT) * dc for x, y, dc in zip(q, k, decay)]
    k_dec = [x * jnp.exp(g[ch - 1:ch, :] - g) for x, g in zip(k, gc)]
    t0 = [state_ref[b, hh] for b, hh in chains]
    v_new = [x - _bdot(y, t) for x, y, t in zip(u, w, t0)]
    out = [_bdot(x * e, t) + _bdot(y, vn) for x, e, t, y, vn in zip(q, eg, t0, qk, v_new)]
    for ci, (b, hh) in enumerate(chains):
        state_ref[b, hh] = t0[ci] * eg[ci][ch - 1:ch, 0:1] + _bdot(k_dec[ci].T, v_new[ci])
    for ci, (b, hh) in enumerate(chains):
        ms = _dotx(out[ci] * out[ci], ones, 2, 1) * (1.0 / GDN_HEAD)
        o = out[ci] * lax.rsqrt(ms + 1e-6) * ng_ref[...]
        si, lo = divmod(hh * GDN_HEAD, sub)
        o_ref[b, :, lane(hh)] = o * _silu(pg_refs[si][b, :, lo:lo + GDN_HEAD])


def gdn_mix(p3, col0, conv_w, a_log, dt_bias, norm_g):
    nb, s, _ = p3.shape
    nheads = a_log.shape[0]
    width = nheads * GDN_HEAD
    wd, sub = GDN_STEP_LANES, GDN_BLOCK_LANES
    nstep, nsub = width // wd, wd // sub
    assert col0 % sub == 0 and width % wd == 0 and wd % sub == 0 and GDN_HEAD == LANES

    def blks(col):
        base = col // sub
        return [pl.BlockSpec((nb, CHUNK, sub), lambda j, c, o=base + si: (0, c, o + j * nsub)) for si in range(nsub)]

    cw = pl.BlockSpec((GDN_CONV, wd), lambda j, c: (0, j))
    small = lambda n: pl.BlockSpec((1, n), lambda j, c: (0, 0))
    return pl.pallas_call(
        functools.partial(_gdn_kernel, nsub),
        out_shape=jax.ShapeDtypeStruct((nb, s, width), F32),
        grid=(nstep, s // CHUNK),
        in_specs=blks(col0) + blks(col0 + width) + blks(col0 + 2 * width) + blks(col0 + 3 * width) + [
            pl.BlockSpec((nb, CHUNK, LANES), lambda j, c: (0, c, (col0 + 4 * width) // LANES)),
            cw, cw, cw, small(nheads), small(nheads), small(LANES)],
        out_specs=pl.BlockSpec((nb, CHUNK, wd), lambda j, c: (0, c, j)),
        scratch_shapes=[pltpu.VMEM((nb, 3, 8 + CHUNK, wd), F32),
                        pltpu.VMEM((nb, wd // GDN_HEAD, LANES, LANES), F32)],
        compiler_params=_cparams(("parallel", "arbitrary")),
        name="gated_deltanet_mixer",
    )(*([p3] * (4 * nsub + 1)), conv_w[:, :width], conv_w[:, width:2 * width], conv_w[:, 2 * width:],
      a_log.reshape(1, -1), dt_bias.reshape(1, -1), norm_g.reshape(1, -1))


HALO_U = 32
HALO_Z = 16
CONV_TILE_ROWS = 128


def _odd_kernel(p_ref, cvw_ref, cvb_ref, lng_ref, lnb_ref, plw_ref, pls_ref, o_ref, ubuf_ref, zbuf_ref, conv_ref):
    s_idx = pl.program_id(1)
    ts = p_ref.shape[1]
    cw = cvb_ref.shape[1]
    pool_w = pls_ref.shape[1]
    pg = pool_w // len(POOL_WINDOWS)

    @pl.when(s_idx == 0)
    def _():
        ubuf_ref[0:HALO_U, :] = jnp.zeros((HALO_U, cw), F32)
        zbuf_ref[0:HALO_Z, :] = jnp.zeros((HALO_Z, pool_w), F32)

    pa = p_ref[0, :, 0:cw]
    pb = p_ref[0, :, cw:2 * cw]
    ubuf_ref[HALO_U:, :] = pa * _sigmoid(pb)
    base = HALO_U - (CONV_KERNEL - 1)
    for r0 in range(0, ts, CONV_TILE_ROWS):
        for c0 in range(0, cw, LANES):
            acc = jnp.broadcast_to(cvb_ref[:, c0:c0 + LANES], (CONV_TILE_ROWS, LANES))
            for j in range(CONV_KERNEL):
                lo = base + j + r0
                acc = acc + ubuf_ref[lo:lo + CONV_TILE_ROWS, c0:c0 + LANES] * cvw_ref[j:j + 1, c0:c0 + LANES]
            conv_ref[r0:r0 + CONV_TILE_ROWS, c0:c0 + LANES] = acc
    ubuf_ref[0:HALO_U, :] = ubuf_ref[ts:ts + HALO_U, :]
    o_ref[0, :, 0:cw] = _silu(_layer_norm(conv_ref[...], lng_ref[...], lnb_ref[...]))

    z = p_ref[0, :, 2 * cw:]
    zbuf_ref[HALO_Z:, :] = z
    t1 = (s_idx * ts + 1 + lax.broadcasted_iota(jnp.int32, (ts, 1), 0)).astype(F32)
    for gi, win in enumerate(POOL_WINDOWS):
        lo = gi * pg
        ssum = None
        for j in range(win):
            t = zbuf_ref[HALO_Z - j:HALO_Z - j + ts, lo:lo + pg]
            ssum = t if ssum is None else ssum + t
        pooled = ssum / jnp.minimum(t1, float(win)) - z[:, lo:lo + pg]
        mixed = _bdot(pooled, plw_ref[gi]) * pls_ref[:, lo:lo + pg]
        o_ref[0, :, cw + lo:cw + lo + pg] = mixed
    zbuf_ref[0:HALO_Z, :] = zbuf_ref[ts:ts + HALO_Z, :]


def odd_mix(p3, cv_w, cv_b, cv_ln_g, cv_ln_b, pl_w, pl_scale, ts=256):
    nb, s, cols = p3.shape
    cw = cv_b.shape[0]
    pool_w = pl_scale.shape[0]
    ts = min(ts, s)
    assert ts % CONV_TILE_ROWS == 0
    row = lambda t: t.reshape(1, -1)
    cvw = jnp.concatenate([cv_w, jnp.zeros((HALO_U - CONV_KERNEL, cw), F32)], 0)
    return pl.pallas_call(
        _odd_kernel,
        out_shape=jax.ShapeDtypeStruct((nb, s, cw + pool_w), F32),
        grid=(nb, s // ts),
        in_specs=[pl.BlockSpec((1, ts, cols), lambda b, i: (b, i, 0)),
                  _const_spec((HALO_U, cw)), _const_spec((1, cw)), _const_spec((1, cw)), _const_spec((1, cw)),
                  _const_spec(pl_w.shape), _const_spec((1, pool_w))],
        out_specs=pl.BlockSpec((1, ts, cw + pool_w), lambda b, i: (b, i, 0)),
        scratch_shapes=[pltpu.VMEM((HALO_U + ts, cw), F32), pltpu.VMEM((HALO_Z + ts, pool_w), F32),
                        pltpu.VMEM((ts, cw), F32)],
        compiler_params=_cparams(("parallel", "arbitrary")),
        name="conformer_pool_mixer",
    )(p3, cvw, row(cv_b), row(cv_ln_g), row(cv_ln_b), pl_w.astype(BF16), row(pl_scale))


def _xattn_kernel(x_ref, k_ref, v_ref, wq_ref, wo_ref, g_ref, b_ref, o_ref):
    x = x_ref[0]
    d = x.shape[1]
    dh = d // XA_HEADS
    q = _bdot(x, wq_ref[...])
    heads = []
    for hd in range(XA_HEADS):
        sl = slice(hd * dh, (hd + 1) * dh)
        s = _bdot(q[:, sl], k_ref[0, :, sl], NT) * (dh ** -0.5)
        s = s - jnp.max(s, axis=-1, keepdims=True)
        e = jnp.exp(s)
        prob = e / jnp.sum(e, axis=-1, keepdims=True)
        heads.append(_bdot(prob, v_ref[0, :, sl]))
    a = _bdot(jnp.concatenate(heads, axis=-1), wo_ref[...])
    o_ref[0] = _layer_norm(DEEPNORM_ALPHA * x + a, g_ref[...], b_ref[...])


def cross_attention_ln(x3, k3, v3, wq, wo, g, b, ts=256):
    nb, s, d = x3.shape
    m = k3.shape[1]
    ts = min(ts, s)
    return pl.pallas_call(
        _xattn_kernel,
        out_shape=jax.ShapeDtypeStruct((nb, s, d), F32),
        grid=(nb, s // ts),
        in_specs=[pl.BlockSpec((1, ts, d), lambda bi, i: (bi, i, 0)),
                  pl.BlockSpec((1, m, d), lambda bi, i: (bi, 0, 0)),
                  pl.BlockSpec((1, m, d), lambda bi, i: (bi, 0, 0)),
                  _const_spec((d, d)), _const_spec((d, d)), _const_spec((1, d)), _const_spec((1, d))],
        out_specs=pl.BlockSpec((1, ts, d), lambda bi, i: (bi, i, 0)),
        compiler_params=_cparams(("parallel", "parallel")),
        name="cross_attention_layernorm",
    )(x3, k3, v3, wq, wo, g.reshape(1, d), b.reshape(1, d))


def _router_kernel(x_ref, wr_ref, br_ref, idx_ref, gate_ref, rank_ref, cnt_ref, base_ref, tri_ref):
    i = pl.program_id(0)
    tm = x_ref.shape[0]
    ne = wr_ref.shape[0]

    @pl.when(i == 0)
    def _():
        base_ref[...] = jnp.zeros_like(base_ref)
        rr = lax.broadcasted_iota(jnp.int32, (tm, tm), 0)
        cc = lax.broadcasted_iota(jnp.int32, (tm, tm), 1)
        tri_ref[...] = (rr <= cc).astype(BF16)

    logits = _dotx(wr_ref[...], x_ref[...], dims=NT) + br_ref[...]
    eidx = lax.broadcasted_iota(jnp.int32, (ne, tm), 0)
    work = logits
    tops, sels = [], []
    for kk in range(TOP_K):
        mx = jnp.max(work, axis=0, keepdims=True)
        sel_idx = jnp.min(jnp.where(work == mx, eidx, ne), axis=0, keepdims=True)
        sel = eidx == sel_idx
        tops.append(mx)
        sels.append(sel)
        idx_ref[kk:kk + 1, :] = sel_idx
        work = jnp.where(sel, -jnp.inf, work)
    es = [jnp.exp(t - tops[0]) for t in tops]
    den = es[0] + es[1] + es[2] + es[3]
    for kk in range(TOP_K):
        gate_ref[kk:kk + 1, :] = es[kk] / den
    onehot = sels[0] | sels[1] | sels[2] | sels[3]
    oh = jnp.where(onehot, 1.0, 0.0)
    incl = jnp.dot(oh.astype(BF16), tri_ref[...], preferred_element_type=F32)
    before = base_ref[:, 0:1] + incl - oh
    for kk in range(TOP_K):
        rank = jnp.sum(jnp.where(sels[kk], before, 0.0), axis=0, keepdims=True)
        rank_ref[kk:kk + 1, :] = rank.astype(jnp.int32)
    total = base_ref[:, 0:1] + incl[:, tm - 1:tm]
    base_ref[...] = jnp.broadcast_to(total, base_ref.shape)
    cnt_ref[...] = jnp.broadcast_to(total, cnt_ref.shape).astype(jnp.int32)


def moe_route(xf, w_r, b_r, tm=512):
    n, d = xf.shape
    ne = w_r.shape[1]
    tm = min(tm, n)
    slot = lambda dt: jax.ShapeDtypeStruct((TOP_K, n), dt)
    return pl.pallas_call(
        _router_kernel,
        out_shape=(slot(jnp.int32), slot(F32), slot(jnp.int32), jax.ShapeDtypeStruct((ne, LANES), jnp.int32)),
        grid=(n // tm,),
        in_specs=[pl.BlockSpec((tm, d), lambda i: (i, 0)), _const_spec((ne, d)), _const_spec((ne, 1))],
        out_specs=(pl.BlockSpec((TOP_K, tm), lambda i: (0, i)), pl.BlockSpec((TOP_K, tm), lambda i: (0, i)),
                   pl.BlockSpec((TOP_K, tm), lambda i: (0, i)), pl.BlockSpec((ne, LANES), lambda i: (0, 0))),
        scratch_shapes=[pltpu.VMEM((ne, LANES), F32), pltpu.VMEM((tm, tm), BF16)],
        compiler_params=_cparams(("arbitrary",)),
        name="moe_router",
    )(xf, w_r.T, b_r.reshape(ne, 1))


def _dispatch_kernel(dest_ref, npad_ref, pstart_ref, nused_ref, x_ref, zero_ref, xs_ref, sem, zsem):
    i = pl.program_id(0)
    tm = x_ref.shape[0]
    ne = npad_ref.shape[0]
    nblk = xs_ref.shape[0] // MOE_ROWS

    @pl.when(i == 0)
    def _():
        def per_expert(e, carry):
            def one(r, c2):
                pltpu.make_async_copy(zero_ref.at[pl.ds(0, 1)], xs_ref.at[pl.ds(pstart_ref[e] + r, 1)], zsem).start()
                return c2
            lax.fori_loop(0, npad_ref[e], one, 0)

            def one_wait(r, c2):
                pltpu.make_async_copy(zero_ref.at[pl.ds(0, 1)], xs_ref.at[pl.ds(0, 1)], zsem).wait()
                return c2
            lax.fori_loop(0, npad_ref[e], one_wait, 0)
            return carry
        lax.fori_loop(0, ne, per_expert, 0)

        def unused(blk, carry):
            cp = pltpu.make_async_copy(zero_ref, xs_ref.at[pl.ds(blk * MOE_ROWS, MOE_ROWS)], zsem)
            cp.start()
            cp.wait()
            return carry
        lax.fori_loop(nused_ref[0], nblk, unused, 0)

    def issue(t, carry):
        for kk in range(TOP_K):
            pltpu.make_async_copy(x_ref.at[pl.ds(t, 1)], xs_ref.at[pl.ds(dest_ref[kk, t], 1)], sem).start(priority=kk % 2)
        return carry
    lax.fori_loop(0, tm, issue, 0)

    def drain(t, carry):
        for kk in range(TOP_K):
            pltpu.make_async_copy(x_ref.at[pl.ds(0, 1)], xs_ref.at[pl.ds(0, 1)], sem).wait()
        return carry
    lax.fori_loop(0, tm, drain, 0)


def moe_dispatch(xf, dest, npad, pad_fill_start, nused, cap, tm=256):
    n, d = xf.shape
    tm = min(tm, n)
    smem = pl.BlockSpec(memory_space=pltpu.SMEM)
    return pl.pallas_call(
        _dispatch_kernel,
        out_shape=jax.ShapeDtypeStruct((cap, d), F32),
        grid=(n // tm,),
        in_specs=[pl.BlockSpec((TOP_K, tm), lambda i: (0, i), memory_space=pltpu.SMEM), smem, smem, smem,
                  pl.BlockSpec((tm, d), lambda i: (i, 0)), _const_spec((MOE_ROWS, d))],
        out_specs=pl.BlockSpec(memory_space=pl.ANY),
        scratch_shapes=[pltpu.SemaphoreType.DMA, pltpu.SemaphoreType.DMA],
        compiler_params=_cparams(("arbitrary",)),
        name="moe_dispatch",
    )(dest, npad, pad_fill_start, nused, xf, jnp.zeros((MOE_ROWS, d), F32))


PAIR_TILE = 2 * LANES


def _expert_changed(be_ref, i):
    return (i == 0) | (be_ref[i] != be_ref[jnp.maximum(i - 1, 0)])


def _next_weights(be_ref, nxt_ref, i, layer, w_hbm, wbuf_ref, sem, use):
    copy = lambda e: pltpu.make_async_copy(w_hbm.at[layer, e], wbuf_ref, sem)
    e = be_ref[i]

    @pl.when(i == 0)
    def _():
        copy(e).start()

    copy(e).wait()
    use()
    nx = nxt_ref[e]

    @pl.when(nx >= 0)
    def _():
        copy(nx).start()


def _expert_kernel(layer, be_ref, nused_ref, nxt_ref, xs_ref, wgu_hbm, bgu_ref, wdn_hbm, bdn_ref, y_ref,
                   wgu_buf, wperm_ref, bperm_ref, wdn_buf, wd_ref, act_ref, sem_gu, sem_dn):
    i = pl.program_id(0)
    active = i < nused_ref[0]
    ntile = wgu_buf.shape[1] // PAIR_TILE

    def deinterleave():
        r = lax.broadcasted_iota(jnp.int32, (PAIR_TILE, PAIR_TILE), 0)
        c = lax.broadcasted_iota(jnp.int32, (PAIR_TILE, PAIR_TILE), 1)
        perm = (r == jnp.where(c < LANES, 2 * c, 2 * (c - LANES) + 1)).astype(BF16)
        for t in range(ntile):
            sl = slice(t * PAIR_TILE, (t + 1) * PAIR_TILE)
            w = wgu_buf[:, sl].astype(BF16)
            wperm_ref[:, sl] = jnp.dot(w, perm, preferred_element_type=F32).astype(BF16)
            bias = jnp.broadcast_to(bgu_ref[0, 0, :, sl], (8, PAIR_TILE))
            bperm_ref[:, sl] = _dotx(bias, perm, 3, 1)

    def cast():
        wd_ref[...] = wdn_buf[...].astype(BF16)

    @pl.when(active & _expert_changed(be_ref, i))
    def _():
        _next_weights(be_ref, nxt_ref, i, layer, wgu_hbm, wgu_buf, sem_gu, deinterleave)
        _next_weights(be_ref, nxt_ref, i, layer, wdn_hbm, wdn_buf, sem_dn, cast)

    @pl.when(active)
    def _():
        x = xs_ref[...].astype(BF16)
        for t in range(ntile):
            sl = slice(t * PAIR_TILE, (t + 1) * PAIR_TILE)
            h = jnp.dot(x, wperm_ref[:, sl], preferred_element_type=F32) + bperm_ref[0:1, sl]
            gate = jnp.minimum(h[:, :LANES], SWIGLU_LIMIT)
            up = jnp.clip(h[:, LANES:], -SWIGLU_LIMIT, SWIGLU_LIMIT)
            act = (up + 1.0) * gate * _sigmoid(SWIGLU_ALPHA * gate)
            act_ref[:, t * LANES:(t + 1) * LANES] = act.astype(BF16)
        y_ref[...] = jnp.dot(act_ref[...], wd_ref[...], preferred_element_type=F32) + bdn_ref[0, 0]

    @pl.when(jnp.logical_not(active))
    def _():
        y_ref[...] = jnp.zeros_like(y_ref)


def moe_experts(xs, block_expert, nused, next_expert, layer, w_gu, b_gu, w_dn, b_dn):
    cap, d = xs.shape
    nl, ne, _, d2 = w_gu.shape
    dff = d2 // 2
    nblk = cap // MOE_ROWS
    blk = lambda i, be, nu, nx: (i, 0)
    bias = lambda n: pl.BlockSpec((1, 1, 1, n), lambda i, be, nu, nx: (layer, be[i], 0, 0))
    hbm = pl.BlockSpec(memory_space=pl.ANY)
    return pl.pallas_call(
        functools.partial(_expert_kernel, layer),
        out_shape=jax.ShapeDtypeStruct((cap, d), F32),
        grid_spec=pltpu.PrefetchScalarGridSpec(
            num_scalar_prefetch=3, grid=(nblk,),
            in_specs=[pl.BlockSpec((MOE_ROWS, d), blk), hbm, bias(d2), hbm, bias(d)],
            out_specs=pl.BlockSpec((MOE_ROWS, d), blk),
            scratch_shapes=[pltpu.VMEM((d, d2), F32), pltpu.VMEM((d, d2), BF16), pltpu.VMEM((8, d2), F32),
                            pltpu.VMEM((dff, d), F32), pltpu.VMEM((dff, d), BF16), pltpu.VMEM((MOE_ROWS, dff), BF16),
                            pltpu.SemaphoreType.DMA, pltpu.SemaphoreType.DMA]),
        compiler_params=_cparams(("arbitrary",)),
        name="moe_experts",
    )(block_expert, nused, next_expert, xs, w_gu, b_gu.reshape(nl, ne, 1, d2), w_dn, b_dn.reshape(nl, ne, 1, d))


def _plan_kernel(idx_ref, rank_ref, cnt_ref, dest_ref):
    ne = cnt_ref.shape[0]
    tm = idx_ref.shape[1]
    padded = ((cnt_ref[...] + (MOE_ROWS - 1)) >> MOE_ROWS_LOG2) << MOE_ROWS_LOG2
    r = lax.broadcasted_iota(jnp.int32, (ne, ne), 0)
    c = lax.broadcasted_iota(jnp.int32, (ne, ne), 1)
    pad_start = _dotx((c < r).astype(BF16), padded.astype(F32), 1, 3)[:, 0:1]
    eidx = lax.broadcasted_iota(jnp.int32, (ne, tm), 0)
    for kk in range(TOP_K):
        start = jnp.sum(jnp.where(eidx == idx_ref[kk:kk + 1, :], pad_start, 0.0), axis=0, keepdims=True)
        dest_ref[kk:kk + 1, :] = rank_ref[kk:kk + 1, :] + start.astype(jnp.int32)


def moe_plan(idx, rank, cnt, tm=2048):
    k, n = idx.shape
    ne = cnt.shape[0]
    tm = min(tm, n)
    spec = pl.BlockSpec((k, tm), lambda i: (0, i))
    return pl.pallas_call(
        _plan_kernel,
        out_shape=jax.ShapeDtypeStruct((k, n), jnp.int32),
        grid=(n // tm,),
        in_specs=[spec, spec, _const_spec((ne, LANES))],
        out_specs=spec,
        compiler_params=_cparams(("parallel",)),
        name="moe_plan",
    )(idx, rank, cnt)


def _combine_kernel(dest_ref, dest_next_ref, y_ref, gate_ref, x_ref, g_ref, b_ref, o_ref, buf_ref, sems):
    i = pl.program_id(0)
    tm = x_ref.shape[0]
    slot = i % 2

    def gather(d_ref, s):
        def issue(t, carry):
            for kk in range(TOP_K):
                pltpu.make_async_copy(y_ref.at[pl.ds(d_ref[kk, t], 1)], buf_ref.at[s, kk, pl.ds(t, 1)],
                                      sems.at[s]).start(priority=kk % 2)
            return carry
        lax.fori_loop(0, tm, issue, 0)

    @pl.when(i == 0)
    def _():
        gather(dest_ref, 0)

    @pl.when(i + 1 < pl.num_programs(0))
    def _():
        gather(dest_next_ref, 1 - slot)

    def drain(t, carry):
        for kk in range(TOP_K):
            pltpu.make_async_copy(y_ref.at[pl.ds(0, 1)], buf_ref.at[slot, kk, pl.ds(0, 1)], sems.at[slot]).wait()
        return carry
    lax.fori_loop(0, tm, drain, 0)

    f = None
    for kk in range(TOP_K):
        t = buf_ref[slot, kk] * gate_ref[:, kk:kk + 1]
        f = t if f is None else f + t
    o_ref[...] = _layer_norm(DEEPNORM_ALPHA * x_ref[...] + f, g_ref[...], b_ref[...])


def moe_combine_ln(y, dest, gate_t, xf, g, b, tm=256):
    n, d = xf.shape
    tm = min(tm, n)
    last = n // tm - 1
    return pl.pallas_call(
        _combine_kernel,
        out_shape=jax.ShapeDtypeStruct((n, d), F32),
        grid=(n // tm,),
        in_specs=[pl.BlockSpec((TOP_K, tm), lambda i: (0, i), memory_space=pltpu.SMEM),
                  pl.BlockSpec((TOP_K, tm), lambda i: (0, jnp.minimum(i + 1, last)), memory_space=pltpu.SMEM),
                  pl.BlockSpec(memory_space=pl.ANY),
                  pl.BlockSpec((tm, TOP_K), lambda i: (i, 0)),
                  pl.BlockSpec((tm, d), lambda i: (i, 0)), _const_spec((1, d)), _const_spec((1, d))],
        out_specs=pl.BlockSpec((tm, d), lambda i: (i, 0)),
        scratch_shapes=[pltpu.VMEM((2, TOP_K, tm, d), F32), pltpu.SemaphoreType.DMA((2,))],
        compiler_params=_cparams(("arbitrary",)),
        name="moe_combine_layernorm",
    )(dest, dest, y, gate_t, xf, g.reshape(1, d), b.reshape(1, d))


def moe_ffn_ln(xf, layer, w_r, b_r, w_gu, b_gu, w_dn, b_dn, g, b):
    n, d = xf.shape
    ne = w_r.shape[1]
    idx, gate, rank, cnt = moe_route(xf, w_r, b_r)
    dest = moe_plan(idx, rank, cnt)
    counts = cnt[:, 0]
    padded = -(-counts // MOE_ROWS) * MOE_ROWS
    pad_end = jnp.cumsum(padded)
    pad_start = pad_end - padded
    cap = n * TOP_K + ne * MOE_ROWS
    nblk = cap // MOE_ROWS
    nused = (pad_end[-1] // MOE_ROWS).astype(jnp.int32).reshape(1)
    blk_row = jnp.arange(nblk, dtype=jnp.int32) * MOE_ROWS
    block_expert = jnp.minimum(jnp.sum(blk_row[:, None] >= pad_end[None, :], axis=1), ne - 1).astype(jnp.int32)
    xs = moe_dispatch(xf, dest, (padded - counts).astype(jnp.int32), (pad_start + counts).astype(jnp.int32),
                      nused, cap)
    eidx = jnp.arange(ne, dtype=jnp.int32)
    later = (eidx[None, :] > eidx[:, None]) & (counts[None, :] > 0)
    next_expert = jnp.min(jnp.where(later, eidx[None, :], ne), axis=1)
    next_expert = jnp.where(next_expert < ne, next_expert, -1).astype(jnp.int32)
    y = moe_experts(xs, block_expert, nused, next_expert, layer, w_gu, b_gu, w_dn, b_dn)
    return moe_combine_ln(y, dest, gate.T, xf, g, b)


def kernel(x, mem, ev_w_in, ev_mu, rk_w0, rk_w2, rk_a0, rk_a2, rk_g2, rk_kk, rk_ka, rk_rk, rk_gn_g, rk_gn_b,
           gd_conv, gd_a_log, gd_dt_bias, gd_norm_g, ev_w_out, od_w_in, cv_w, cv_b, cv_ln_g, cv_ln_b, pl_w,
           pl_scale, od_w_out, xa_wq, xa_wk, xa_wv, xa_wo, moe_wr, moe_br, moe_wgu, moe_bgu, moe_wdn, moe_bdn,
           ln_g, ln_b):
    nb, s, d = x.shape
    n = nb * s
    m = mem.shape[1]
    xf = x.reshape(n, d)
    memf = mem.reshape(nb * m, d)
    for layer in range(DEPTH):
        i = layer // 2
        if layer % 2 == 0:
            rwkv_cols = ev_mu.shape[1]
            p3 = matmul(xf, ev_w_in, i, tm=IN_PROJ_ROWS).reshape(nb, s, -1)
            ya = rwkv_mix(p3, ev_mu[i], rk_w0[i], rk_w2[i], rk_a0[i], rk_a2[i], rk_g2[i], rk_kk[i], rk_ka[i],
                          rk_rk[i].reshape(-1), rk_gn_g[i], rk_gn_b[i])
            yb = gdn_mix(p3, rwkv_cols, gd_conv[i], gd_a_log[i], gd_dt_bias[i], gd_norm_g[i])
            mixed = [ya.reshape(n, -1), yb.reshape(n, -1)]
            w_out = ev_w_out[i]
        else:
            p3 = matmul(xf, od_w_in, i, tm=IN_PROJ_ROWS).reshape(nb, s, -1)
            mixed = [odd_mix(p3, cv_w[i], cv_b[i], cv_ln_g[i], cv_ln_b[i], pl_w[i], pl_scale[i]).reshape(n, -1)]
            w_out = od_w_out[i]
        xf = mm_res_ln(mixed, w_out.astype(BF16), xf, ln_g[layer, 0], ln_b[layer, 0])
        k3 = matmul(memf, xa_wk, layer).reshape(nb, m, d)
        v3 = matmul(memf, xa_wv, layer).reshape(nb, m, d)
        xf = cross_attention_ln(xf.reshape(nb, s, d), k3, v3, xa_wq[layer].astype(BF16), xa_wo[layer].astype(BF16),
                                ln_g[layer, 1], ln_b[layer, 1]).reshape(n, d)
        xf = moe_ffn_ln(xf, layer, moe_wr[layer], moe_br[layer], moe_wgu, moe_bgu, moe_wdn, moe_bdn,
                        ln_g[layer, 2], ln_b[layer, 2])
    return xf.reshape(nb, s, d)
```

```python
import functools
import math

import jax
import jax.numpy as jnp
from jax import lax
from jax.experimental import pallas as pl
from jax.experimental.pallas import tpu as pltpu

F32 = jnp.float32
BF16 = jnp.bfloat16

LANES = 128
VMEM_LIMIT = 56 * 1024 * 1024

DEPTH = 2
DEEPNORM_ALPHA = (2 * DEPTH) ** 0.25
LN_EPS = 1e-5
CHUNK = 64
RWKV_HEAD = 64
RWKV_GN_EPS = 64e-5
GDN_HEAD = 128
GDN_CONV = 4
CONV_KERNEL = 31
POOL_WINDOWS = (2, 4, 8, 16)
XA_HEADS = 4
N_EXPERTS = 32
TOP_K = 4
SWIGLU_LIMIT = 7.0
SWIGLU_ALPHA = 1.702
IN_PROJ_ROWS = 2048
MOE_ROWS_LOG2 = 8
MOE_ROWS = 1 << MOE_ROWS_LOG2

NN = (((1,), (0,)), ((), ()))
NT = (((1,), (1,)), ((), ()))


def _cparams(sem):
    return pltpu.CompilerParams(dimension_semantics=sem, vmem_limit_bytes=VMEM_LIMIT)


def _bdot(a, b, dims=NN):
    return lax.dot_general(a.astype(BF16), b.astype(BF16), dims, preferred_element_type=F32)


def _parts(x, n):
    out, rem = [], x
    for i in range(n):
        h = rem.astype(BF16)
        out.append(h)
        if i + 1 < n:
            rem = rem - h.astype(F32)
    return out


def _dotx(a, b, na=2, nb=2, dims=NN):
    ap, bp = _parts(a, na), _parts(b, nb)
    acc = None
    for i, ai in enumerate(ap):
        for j, bj in enumerate(bp):
            if i + j < max(na, nb):
                t = lax.dot_general(ai, bj, dims, preferred_element_type=F32)
                acc = t if acc is None else acc + t
    return acc


def _layer_norm(v, g, b):
    mu = jnp.mean(v, axis=-1, keepdims=True)
    c = v - mu
    var = jnp.mean(c * c, axis=-1, keepdims=True)
    return c * lax.rsqrt(var + LN_EPS) * g + b


def _sigmoid(x):
    return jax.nn.sigmoid(x)


def _silu(x):
    return x * jax.nn.sigmoid(x)


def _const_spec(shape):
    return pl.BlockSpec(shape, lambda *_: (0,) * len(shape), pipeline_mode=pl.Buffered(1))


def _mm_kernel(x_ref, w_ref, o_ref):
    o_ref[...] = _bdot(x_ref[...], w_ref[0])


def matmul(x, w_stack, idx, tm=1024, tn=512):
    m, k = x.shape
    n = w_stack.shape[2]
    tm = min(tm, m)
    return pl.pallas_call(
        _mm_kernel,
        out_shape=jax.ShapeDtypeStruct((m, n), F32),
        grid=(pl.cdiv(m, tm), pl.cdiv(n, tn)),
        in_specs=[pl.BlockSpec((tm, k), lambda i, j: (i, 0)),
                  pl.BlockSpec((1, k, tn), lambda i, j: (idx, 0, j))],
        out_specs=pl.BlockSpec((tm, tn), lambda i, j: (i, j)),
        compiler_params=_cparams(("parallel", "arbitrary")),
        name="dense_matmul",
    )(x, w_stack)


def _mm_res_ln_kernel(n_in, *refs):
    a_refs = refs[:n_in]
    w_refs = refs[n_in:2 * n_in]
    x_ref, g_ref, b_ref, o_ref = refs[2 * n_in:]
    h = None
    for a_ref, w_ref in zip(a_refs, w_refs):
        t = _bdot(a_ref[...], w_ref[...])
        h = t if h is None else h + t
    o_ref[...] = _layer_norm(DEEPNORM_ALPHA * x_ref[...] + h, g_ref[...], b_ref[...])


def mm_res_ln(a_list, w, x, g, b, tm=256):
    m, d = x.shape
    tm = min(tm, m)
    in_specs, off = [], 0
    for a in a_list:
        in_specs.append(pl.BlockSpec((tm, a.shape[1]), lambda i: (i, 0)))
    w_parts = []
    for a in a_list:
        ka = a.shape[1]
        w_parts.append(lax.slice_in_dim(w, off, off + ka, axis=0))
        in_specs.append(_const_spec((ka, d)))
        off += ka
    in_specs += [pl.BlockSpec((tm, d), lambda i: (i, 0)), _const_spec((1, d)), _const_spec((1, d))]
    return pl.pallas_call(
        functools.partial(_mm_res_ln_kernel, len(a_list)),
        out_shape=jax.ShapeDtypeStruct((m, d), F32),
        grid=(m // tm,),
        in_specs=in_specs,
        out_specs=pl.BlockSpec((tm, d), lambda i: (i, 0)),
        compiler_params=_cparams(("parallel",)),
        name="proj_residual_layernorm",
    )(*a_list, *w_parts, x, g.reshape(1, d), b.reshape(1, d))


RWKV_STEP_LANES = 8 * LANES
GDN_STEP_LANES = 8 * LANES
GDN_BLOCK_LANES = 2 * LANES


def _group_sum(x, gmat):
    return _dotx(x, gmat, 2, 1)


def _rwkv_kernel(pr_ref, pk_ref, pv_ref, pl_ref, mur_ref, muk_ref, muv_ref, mul_ref,
                 w0_ref, w2_ref, a0_ref, a2_ref, g2_ref, kk_ref, ka_ref, rk_ref, gng_ref, gnb_ref,
                 o_ref, prev_ref, prevl_ref, state_ref):
    c = pl.program_id(1)
    nb, ch, wd = pr_ref.shape

    @pl.when(c == 0)
    def _():
        prev_ref[...] = jnp.zeros_like(prev_ref)
        prevl_ref[...] = jnp.zeros_like(prevl_ref)
        state_ref[...] = jnp.zeros_like(state_ref)

    col = lax.broadcasted_iota(jnp.int32, (ch, LANES), 1)
    mlo = (col < RWKV_HEAD).astype(F32)
    mhi = 1.0 - mlo
    r2 = lax.broadcasted_iota(jnp.int32, (2 * ch, 2 * ch), 0)
    c2 = lax.broadcasted_iota(jnp.int32, (2 * ch, 2 * ch), 1)
    same = (r2 // ch) == (c2 // ch)
    strict = same & ((r2 % ch) > (c2 % ch))
    incl = same & ((r2 % ch) >= (c2 % ch))
    eye = (r2 == c2).astype(F32)
    rg = lax.broadcasted_iota(jnp.int32, (wd, wd), 0)
    cg = lax.broadcasted_iota(jnp.int32, (wd, wd), 1)
    gmat = ((rg // RWKV_HEAD) == (cg // RWKV_HEAD)).astype(BF16)
    rr = lax.broadcasted_iota(jnp.int32, (ch, ch), 0)
    cc = lax.broadcasted_iota(jnp.int32, (ch, ch), 1)
    tri = (rr >= cc).astype(BF16)

    def shifted(x, prev):
        return jnp.where(lax.broadcasted_iota(jnp.int32, x.shape, 0) == 0, prev, pltpu.roll(x, 1, 0))

    def stack(x):
        return jnp.concatenate([x * mlo, x * mhi], axis=0)

    pre = []
    for b in range(nb):
        raw = [pr_ref[b], pk_ref[b], pv_ref[b]]
        rawl = pl_ref[b]
        mus = [mur_ref[...], muk_ref[...], muv_ref[...]]
        mixed = []
        for i in range(3):
            xs = shifted(raw[i], prev_ref[b, i])
            mixed.append(raw[i] + (xs - raw[i]) * mus[i])
        xsl = shifted(rawl, prevl_ref[b])
        lora = rawl + (xsl - rawl) * mul_ref[...]
        for i in range(3):
            prev_ref[b, i] = raw[i][ch - 1:ch]
        prevl_ref[b] = rawl[ch - 1:ch]
        pre.append((mixed, lora))

    elem = []
    for b in range(nb):
        (r, k, v), lora = pre[b]
        l1, l2 = lora[:, :LANES], lora[:, LANES:]
        w = w0_ref[...] + _bdot(jnp.tanh(l1), w2_ref[...])
        ld = -math.exp(-0.5) * _sigmoid(w)
        a = _sigmoid(a0_ref[...] + _bdot(l1, a2_ref[...]))
        g = _bdot(_sigmoid(l2), g2_ref[...])
        kraw = k * kk_ref[...]
        kk = kraw * lax.rsqrt(_group_sum(kraw * kraw, gmat) + 1e-6)
        kmod = k * (1.0 + (a - 1.0) * ka_ref[...])
        cum = _dotx(tri, ld, 1, 3)
        w_t = jnp.exp(cum)
        w_prev = jnp.exp(cum - ld)
        w_inv = jnp.exp(-cum)
        w_end = jnp.exp(cum[ch - 1:ch])
        beta = kk * a
        fb = beta * w_inv
        fk = kmod * w_inv
        elem.append(dict(r=r, v=v, g=g, kmod=kmod, w_end=w_end, fa=-kk * w_prev, fb=fb, fk=fk, fq=r * w_t,
                         fbh=fb * w_end, fkh=fk * w_end))

    chains = [(b, hp) for b in range(nb) for hp in range(wd // LANES)]
    lane = lambda hp: slice(hp * LANES, (hp + 1) * LANES)
    take = lambda name: [stack(elem[b][name][:, lane(hp)]) for b, hp in chains]
    a2, b2, k2, q2, v2, bh2, kh2 = (take(n) for n in ("fa", "fb", "fk", "fq", "v", "fbh", "fkh"))
    aq = [jnp.concatenate([x, y], axis=0) for x, y in zip(a2, q2)]
    bk = [jnp.concatenate([x, y], axis=0) for x, y in zip(b2, k2)]
    sc = [_bdot(x, y, dims=NT) for x, y in zip(aq, bk)]
    n2 = 2 * ch
    lab = [jnp.where(strict, x[:n2, :n2], 0.0) for x in sc]
    lak = [jnp.where(strict, x[:n2, n2:], 0.0) for x in sc]
    grb = [jnp.where(incl, x[n2:, :n2], 0.0) for x in sc]
    grk = [jnp.where(incl, x[n2:, n2:], 0.0) for x in sc]
    minv = [eye + x for x in lab]
    pw = lab
    for _ in range(5):
        pw = [_bdot(x, x) for x in pw]
        minv = [m + _bdot(m, x) for m, x in zip(minv, pw)]
    t0 = [state_ref[b, hp] for b, hp in chains]
    at = [_bdot(x, t) for x, t in zip(aq, t0)]
    rhs = [x[:n2] + _bdot(l, vv) for x, l, vv in zip(at, lak, v2)]
    u2 = [_bdot(m, x) for m, x in zip(minv, rhs)]
    uv = [jnp.concatenate([u, vv], axis=0) for u, vv in zip(u2, v2)]
    lhs = [jnp.concatenate([jnp.concatenate([gb, gk], axis=1), jnp.concatenate([bh.T, kh.T], axis=1)], axis=0)
           for gb, gk, bh, kh in zip(grb, grk, bh2, kh2)]
    res = [_bdot(x, y) for x, y in zip(lhs, uv)]
    y2 = [x[n2:] + r[:n2] for x, r in zip(at, res)]
    for ci, (b, hp) in enumerate(chains):
        state_ref[b, hp] = elem[b]["w_end"][:, lane(hp)].T * t0[ci] + res[ci][n2:]

    for b in range(nb):
        e = elem[b]
        y = jnp.concatenate([y2[ci][:ch] + y2[ci][ch:] for ci, (bb, hp) in enumerate(chains) if bb == b], axis=1)
        inv_n = 1.0 / RWKV_HEAD
        mean = _group_sum(y, gmat) * inv_n
        yc = y - mean
        var = _group_sum(yc * yc, gmat) * inv_n
        yn = yc * lax.rsqrt(var + RWKV_GN_EPS) * gng_ref[...] + gnb_ref[...]
        bonus = _group_sum(e["r"] * e["kmod"] * rk_ref[...], gmat) * e["v"]
        o_ref[b] = (yn + bonus) * e["g"]


def rwkv_mix(p3, mu, w0, w2, a0, a2, g2, k_k, k_a, r_k, gn_g, gn_b):
    nb, s, _ = p3.shape
    width = w0.shape[0]
    wd = RWKV_STEP_LANES
    nstep = width // wd
    rank_w, rank_a = w2.shape[0], a2.shape[0]
    assert rank_w + rank_a == LANES and g2.shape[0] == LANES and 2 * CHUNK == LANES and 2 * RWKV_HEAD == LANES
    assert width % wd == 0 and (3 * width) % (2 * LANES) == 0
    w2p = jnp.concatenate([w2, jnp.zeros((rank_a, width), F32)], 0)
    a2p = jnp.concatenate([jnp.zeros((rank_w, width), F32), a2], 0)
    row = lambda t: t.reshape(1, -1)
    blk = lambda off: pl.BlockSpec((nb, CHUNK, wd), lambda j, c: (0, c, off + j))
    vec = lambda off: pl.BlockSpec((1, wd), lambda j, c: (0, off + j))
    mat = pl.BlockSpec((LANES, wd), lambda j, c: (0, j))
    lora_blk = 3 * width // (2 * LANES)
    mu2 = row(mu)
    return pl.pallas_call(
        _rwkv_kernel,
        out_shape=jax.ShapeDtypeStruct((nb, s, width), F32),
        grid=(nstep, s // CHUNK),
        in_specs=[blk(0), blk(nstep), blk(2 * nstep),
                  pl.BlockSpec((nb, CHUNK, 2 * LANES), lambda j, c: (0, c, lora_blk)),
                  vec(0), vec(nstep), vec(2 * nstep),
                  pl.BlockSpec((1, 2 * LANES), lambda j, c: (0, lora_blk)),
                  vec(0), mat, vec(0), mat, mat, vec(0), vec(0), vec(0), vec(0), vec(0)],
        out_specs=pl.BlockSpec((nb, CHUNK, wd), lambda j, c: (0, c, j)),
        scratch_shapes=[pltpu.VMEM((nb, 3, 1, wd), F32),
                        pltpu.VMEM((nb, 1, 2 * LANES), F32),
                        pltpu.VMEM((nb, wd // LANES, LANES, LANES), F32)],
        compiler_params=_cparams(("parallel", "arbitrary")),
        name="rwkv7_mixer",
    )(p3, p3, p3, p3, mu2, mu2, mu2, mu2, row(w0), w2p, row(a0), a2p, g2,
      row(k_k), row(k_a), row(r_k), row(gn_g), row(gn_b))


def _gdn_kernel(nsub, *refs):
    pq_refs, pk_refs, pv_refs, pg_refs = (refs[i * nsub:(i + 1) * nsub] for i in range(4))
    pbd_ref, cq_ref, ck_ref, cv_ref, alog_ref, dtb_ref, ng_ref, o_ref, halo_ref, state_ref = refs[4 * nsub:]
    j = pl.program_id(0)
    c = pl.program_id(1)
    nb, ch, sub = pq_refs[0].shape
    wd = nsub * sub
    nheads = alog_ref.shape[1]
    hps = wd // GDN_HEAD

    @pl.when(c == 0)
    def _():
        halo_ref[...] = jnp.zeros_like(halo_ref)
        state_ref[...] = jnp.zeros_like(state_ref)

    col = lax.broadcasted_iota(jnp.int32, (ch, LANES), 1)
    rr = lax.broadcasted_iota(jnp.int32, (ch, ch), 0)
    cc = lax.broadcasted_iota(jnp.int32, (ch, ch), 1)
    causal = rr >= cc
    strict = rr > cc
    tri = causal.astype(BF16)
    triu = (rr <= cc).astype(BF16)
    eye = (rr == cc).astype(F32)
    ones = jnp.ones((LANES, LANES), BF16)
    hcol = lax.broadcasted_iota(jnp.int32, (1, nheads), 1)

    def conv_silu(x_refs, w_ref, b, i):
        for si, x_ref in enumerate(x_refs):
            halo_ref[b, i, 8:, si * sub:(si + 1) * sub] = x_ref[b]
        acc = None
        for t in range(GDN_CONV):
            lo = 8 - (GDN_CONV - 1) + t
            term = halo_ref[b, i, lo:lo + ch, :] * w_ref[t:t + 1, :]
            acc = term if acc is None else acc + term
        halo_ref[b, i, 0:8, :] = halo_ref[b, i, ch:ch + 8, :]
        return _silu(acc)

    conv = [(conv_silu(pq_refs, cq_ref, b, 0), conv_silu(pk_refs, ck_ref, b, 1), conv_silu(pv_refs, cv_ref, b, 2))
            for b in range(nb)]
    chains = [(b, hh) for b in range(nb) for hh in range(hps)]
    lane = lambda hh: slice(hh * GDN_HEAD, (hh + 1) * GDN_HEAD)
    q = [conv[b][0][:, lane(hh)] for b, hh in chains]
    k = [conv[b][1][:, lane(hh)] for b, hh in chains]
    v = [conv[b][2][:, lane(hh)] for b, hh in chains]
    q = [x * lax.rsqrt(_dotx(x * x, ones, 2, 1) + 1e-6) * (GDN_HEAD ** -0.5) for x in q]
    k = [x * lax.rsqrt(_dotx(x * x, ones, 2, 1) + 1e-6) for x in k]
    beta, la = [], []
    for b, hh in chains:
        h = j * hps + hh
        bd = pbd_ref[b]
        a_coef = -jnp.exp(jnp.sum(jnp.where(hcol == h, alog_ref[...], 0.0), axis=-1, keepdims=True))
        dt_b = jnp.sum(jnp.where(hcol == h, dtb_ref[...], 0.0), axis=-1, keepdims=True)
        bcol = jnp.sum(jnp.where(col == h, bd, 0.0), axis=-1, keepdims=True)
        dcol = jnp.sum(jnp.where(col == h + nheads, bd, 0.0), axis=-1, keepdims=True)
        beta.append(_sigmoid(bcol))
        z = dcol + dt_b
        softplus = jnp.maximum(z, 0.0) + jnp.log(1.0 + jnp.exp(-jnp.abs(z)))
        la.append(jnp.broadcast_to(a_coef * softplus, (ch, LANES)))
    gc = [_dotx(tri, x, 1, 3) for x in la]
    gc_row = [_dotx(jnp.broadcast_to(x.T[0:1, :], (ch, ch)), triu, 3, 1) for x in la]
    decay = [jnp.where(causal, jnp.exp(jnp.where(causal, g[:, :ch] - gr, 0.0)), 0.0) for g, gr in zip(gc, gc_row)]
    eg = [jnp.exp(g) for g in gc]
    kb = [x * bt for x, bt in zip(k, beta)]
    lower = [jnp.where(strict, _bdot(x, y, dims=NT) * dc, 0.0) for x, y, dc in zip(kb, k, decay)]
    tinv = [eye - x for x in lower]
    pw = [-x for x in lower]
    for _ in range(5):
        pw = [_bdot(x, x) for x in pw]
        tinv = [t + _bdot(t, x) for t, x in zip(tinv, pw)]
    u = [_bdot(t, x * bt) for t, x, bt in zip(tinv, v, beta)]
    w = [_bdot(t, x * e) for t, x, e in zip(tinv, kb, eg)]
    qk = [_bdot(x, y, dims=NT) * dc for x, y, dc in zip(q, k, decay)]
    k_dec = [x * jnp.exp(g[ch - 1:ch, :] - g) for x, g in zip(k, gc)]
    t0 = [state_ref[b, hh] for b, hh in chains]
    v_new = [x - _bdot(y, t) for x, y, t in zip(u, w, t0)]
    out = [_bdot(x * e, t) + _bdot(y, vn) for x, e, t, y, vn in zip(q, eg, t0, qk, v_new)]
    for ci, (b, hh) in enumerate(chains):
        state_ref[b, hh] = t0[ci] * eg[ci][ch - 1:ch, 0:1] + _bdot(k_dec[ci].T, v_new[ci])
    for ci, (b, hh) in enumerate(chains):
        ms = _dotx(out[ci] * out[ci], ones, 2, 1) * (1.0 / GDN_HEAD)
        o = out[ci] * lax.rsqrt(ms + 1e-6) * ng_ref[...]
        si, lo = divmod(hh * GDN_HEAD, sub)
        o_ref[b, :, lane(hh)] = o * _silu(pg_refs[si][b, :, lo:lo + GDN_HEAD])


def gdn_mix(p3, col0, conv_w, a_log, dt_bias, norm_g):
    nb, s, _ = p3.shape
    nheads = a_log.shape[0]
    width = nheads * GDN_HEAD
    wd, sub = GDN_STEP_LANES, GDN_BLOCK_LANES
    nstep, nsub = width // wd, wd // sub
    assert col0 % sub == 0 and width % wd == 0 and wd % sub == 0 and GDN_HEAD == LANES

    def blks(col):
        base = col // sub
        return [pl.BlockSpec((nb, CHUNK, sub), lambda j, c, o=base + si: (0, c, o + j * nsub)) for si in range(nsub)]

    cw = pl.BlockSpec((GDN_CONV, wd), lambda j, c: (0, j))
    small = lambda n: pl.BlockSpec((1, n), lambda j, c: (0, 0))
    return pl.pallas_call(
        functools.partial(_gdn_kernel, nsub),
        out_shape=jax.ShapeDtypeStruct((nb, s, width), F32),
        grid=(nstep, s // CHUNK),
        in_specs=blks(col0) + blks(col0 + width) + blks(col0 + 2 * width) + blks(col0 + 3 * width) + [
            pl.BlockSpec((nb, CHUNK, LANES), lambda j, c: (0, c, (col0 + 4 * width) // LANES)),
            cw, cw, cw, small(nheads), small(nheads), small(LANES)],
        out_specs=pl.BlockSpec((nb, CHUNK, wd), lambda j, c: (0, c, j)),
        scratch_shapes=[pltpu.VMEM((nb, 3, 8 + CHUNK, wd), F32),
                        pltpu.VMEM((nb, wd // GDN_HEAD, LANES, LANES), F32)],
        compiler_params=_cparams(("parallel", "arbitrary")),
        name="gated_deltanet_mixer",
    )(*([p3] * (4 * nsub + 1)), conv_w[:, :width], conv_w[:, width:2 * width], conv_w[:, 2 * width:],
      a_log.reshape(1, -1), dt_bias.reshape(1, -1), norm_g.reshape(1, -1))


HALO_U = 32
HALO_Z = 16
CONV_TILE_ROWS = 128


def _odd_kernel(p_ref, cvw_ref, cvb_ref, lng_ref, lnb_ref, plw_ref, pls_ref, o_ref, ubuf_ref, zbuf_ref, conv_ref):
    s_idx = pl.program_id(1)
    ts = p_ref.shape[1]
    cw = cvb_ref.shape[1]
    pool_w = pls_ref.shape[1]
    pg = pool_w // len(POOL_WINDOWS)

    @pl.when(s_idx == 0)
    def _():
        ubuf_ref[0:HALO_U, :] = jnp.zeros((HALO_U, cw), F32)
        zbuf_ref[0:HALO_Z, :] = jnp.zeros((HALO_Z, pool_w), F32)

    pa = p_ref[0, :, 0:cw]
    pb = p_ref[0, :, cw:2 * cw]
    ubuf_ref[HALO_U:, :] = pa * _sigmoid(pb)
    base = HALO_U - (CONV_KERNEL - 1)
    for r0 in range(0, ts, CONV_TILE_ROWS):
        for c0 in range(0, cw, LANES):
            acc = jnp.broadcast_to(cvb_ref[:, c0:c0 + LANES], (CONV_TILE_ROWS, LANES))
            for j in range(CONV_KERNEL):
                lo = base + j + r0
                acc = acc + ubuf_ref[lo:lo + CONV_TILE_ROWS, c0:c0 + LANES] * cvw_ref[j:j + 1, c0:c0 + LANES]
            conv_ref[r0:r0 + CONV_TILE_ROWS, c0:c0 + LANES] = acc
    ubuf_ref[0:HALO_U, :] = ubuf_ref[ts:ts + HALO_U, :]
    o_ref[0, :, 0:cw] = _silu(_layer_norm(conv_ref[...], lng_ref[...], lnb_ref[...]))

    z = p_ref[0, :, 2 * cw:]
    zbuf_ref[HALO_Z:, :] = z
    t1 = (s_idx * ts + 1 + lax.broadcasted_iota(jnp.int32, (ts, 1), 0)).astype(F32)
    for gi, win in enumerate(POOL_WINDOWS):
        lo = gi * pg
        ssum = None
        for j in range(win):
            t = zbuf_ref[HALO_Z - j:HALO_Z - j + ts, lo:lo + pg]
            ssum = t if ssum is None else ssum + t
        pooled = ssum / jnp.minimum(t1, float(win)) - z[:, lo:lo + pg]
        mixed = _bdot(pooled, plw_ref[gi]) * pls_ref[:, lo:lo + pg]
        o_ref[0, :, cw + lo:cw + lo + pg] = mixed
    zbuf_ref[0:HALO_Z, :] = zbuf_ref[ts:ts + HALO_Z, :]


def odd_mix(p3, cv_w, cv_b, cv_ln_g, cv_ln_b, pl_w, pl_scale, ts=256):
    nb, s, cols = p3.shape
    cw = cv_b.shape[0]
    pool_w = pl_scale.shape[0]
    ts = min(ts, s)
    assert ts % CONV_TILE_ROWS == 0
    row = lambda t: t.reshape(1, -1)
    cvw = jnp.concatenate([cv_w, jnp.zeros((HALO_U - CONV_KERNEL, cw), F32)], 0)
    return pl.pallas_call(
        _odd_kernel,
        out_shape=jax.ShapeDtypeStruct((nb, s, cw + pool_w), F32),
        grid=(nb, s // ts),
        in_specs=[pl.BlockSpec((1, ts, cols), lambda b, i: (b, i, 0)),
                  _const_spec((HALO_U, cw)), _const_spec((1, cw)), _const_spec((1, cw)), _const_spec((1, cw)),
                  _const_spec(pl_w.shape), _const_spec((1, pool_w))],
        out_specs=pl.BlockSpec((1, ts, cw + pool_w), lambda b, i: (b, i, 0)),
        scratch_shapes=[pltpu.VMEM((HALO_U + ts, cw), F32), pltpu.VMEM((HALO_Z + ts, pool_w), F32),
                        pltpu.VMEM((ts, cw), F32)],
        compiler_params=_cparams(("parallel", "arbitrary")),
        name="conformer_pool_mixer",
    )(p3, cvw, row(cv_b), row(cv_ln_g), row(cv_ln_b), pl_w.astype(BF16), row(pl_scale))


def _xattn_kernel(x_ref, k_ref, v_ref, wq_ref, wo_ref, g_ref, b_ref, o_ref):
    x = x_ref[0]
    d = x.shape[1]
    dh = d // XA_HEADS
    q = _bdot(x, wq_ref[...])
    heads = []
    for hd in range(XA_HEADS):
        sl = slice(hd * dh, (hd + 1) * dh)
        s = _bdot(q[:, sl], k_ref[0, :, sl], NT) * (dh ** -0.5)
        s = s - jnp.max(s, axis=-1, keepdims=True)
        e = jnp.exp(s)
        prob = e / jnp.sum(e, axis=-1, keepdims=True)
        heads.append(_bdot(prob, v_ref[0, :, sl]))
    a = _bdot(jnp.concatenate(heads, axis=-1), wo_ref[...])
    o_ref[0] = _layer_norm(DEEPNORM_ALPHA * x + a, g_ref[...], b_ref[...])


def cross_attention_ln(x3, k3, v3, wq, wo, g, b, ts=256):
    nb, s, d = x3.shape
    m = k3.shape[1]
    ts = min(ts, s)
    return pl.pallas_call(
        _xattn_kernel,
        out_shape=jax.ShapeDtypeStruct((nb, s, d), F32),
        grid=(nb, s // ts),
        in_specs=[pl.BlockSpec((1, ts, d), lambda bi, i: (bi, i, 0)),
                  pl.BlockSpec((1, m, d), lambda bi, i: (bi, 0, 0)),
                  pl.BlockSpec((1, m, d), lambda bi, i: (bi, 0, 0)),
                  _const_spec((d, d)), _const_spec((d, d)), _const_spec((1, d)), _const_spec((1, d))],
        out_specs=pl.BlockSpec((1, ts, d), lambda bi, i: (bi, i, 0)),
        compiler_params=_cparams(("parallel", "parallel")),
        name="cross_attention_layernorm",
    )(x3, k3, v3, wq, wo, g.reshape(1, d), b.reshape(1, d))


def _router_kernel(x_ref, wr_ref, br_ref, idx_ref, gate_ref, rank_ref, cnt_ref, base_ref, tri_ref):
    i = pl.program_id(0)
    tm = x_ref.shape[0]
    ne = wr_ref.shape[0]

    @pl.when(i == 0)
    def _():
        base_ref[...] = jnp.zeros_like(base_ref)
        rr = lax.broadcasted_iota(jnp.int32, (tm, tm), 0)
        cc = lax.broadcasted_iota(jnp.int32, (tm, tm), 1)
        tri_ref[...] = (rr <= cc).astype(BF16)

    logits = _dotx(wr_ref[...], x_ref[...], dims=NT) + br_ref[...]
    eidx = lax.broadcasted_iota(jnp.int32, (ne, tm), 0)
    work = logits
    tops, sels = [], []
    for kk in range(TOP_K):
        mx = jnp.max(work, axis=0, keepdims=True)
        sel_idx = jnp.min(jnp.where(work == mx, eidx, ne), axis=0, keepdims=True)
        sel = eidx == sel_idx
        tops.append(mx)
        sels.append(sel)
        idx_ref[kk:kk + 1, :] = sel_idx
        work = jnp.where(sel, -jnp.inf, work)
    es = [jnp.exp(t - tops[0]) for t in tops]
    den = es[0] + es[1] + es[2] + es[3]
    for kk in range(TOP_K):
        gate_ref[kk:kk + 1, :] = es[kk] / den
    onehot = sels[0] | sels[1] | sels[2] | sels[3]
    oh = jnp.where(onehot, 1.0, 0.0)
    incl = jnp.dot(oh.astype(BF16), tri_ref[...], preferred_element_type=F32)
    before = base_ref[:, 0:1] + incl - oh
    for kk in range(TOP_K):
        rank = jnp.sum(jnp.where(sels[kk], before, 0.0), axis=0, keepdims=True)
        rank_ref[kk:kk + 1, :] = rank.astype(jnp.int32)
    total = base_ref[:, 0:1] + incl[:, tm - 1:tm]
    base_ref[...] = jnp.broadcast_to(total, base_ref.shape)
    cnt_ref[...] = jnp.broadcast_to(total, cnt_ref.shape).astype(jnp.int32)


def moe_route(xf, w_r, b_r, tm=512):
    n, d = xf.shape
    ne = w_r.shape[1]
    tm = min(tm, n)
    slot = lambda dt: jax.ShapeDtypeStruct((TOP_K, n), dt)
    return pl.pallas_call(
        _router_kernel,
        out_shape=(slot(jnp.int32), slot(F32), slot(jnp.int32), jax.ShapeDtypeStruct((ne, LANES), jnp.int32)),
        grid=(n // tm,),
        in_specs=[pl.BlockSpec((tm, d), lambda i: (i, 0)), _const_spec((ne, d)), _const_spec((ne, 1))],
        out_specs=(pl.BlockSpec((TOP_K, tm), lambda i: (0, i)), pl.BlockSpec((TOP_K, tm), lambda i: (0, i)),
                   pl.BlockSpec((TOP_K, tm), lambda i: (0, i)), pl.BlockSpec((ne, LANES), lambda i: (0, 0))),
        scratch_shapes=[pltpu.VMEM((ne, LANES), F32), pltpu.VMEM((tm, tm), BF16)],
        compiler_params=_cparams(("arbitrary",)),
        name="moe_router",
    )(xf, w_r.T, b_r.reshape(ne, 1))


def _dispatch_kernel(dest_ref, npad_ref, pstart_ref, nused_ref, x_ref, zero_ref, xs_ref, sem, zsem):
    i = pl.program_id(0)
    tm = x_ref.shape[0]
    ne = npad_ref.shape[0]
    nblk = xs_ref.shape[0] // MOE_ROWS

    @pl.when(i == 0)
    def _():
        def per_expert(e, carry):
            def one(r, c2):
                pltpu.make_async_copy(zero_ref.at[pl.ds(0, 1)], xs_ref.at[pl.ds(pstart_ref[e] + r, 1)], zsem).start()
                return c2
            lax.fori_loop(0, npad_ref[e], one, 0)

            def one_wait(r, c2):
                pltpu.make_async_copy(zero_ref.at[pl.ds(0, 1)], xs_ref.at[pl.ds(0, 1)], zsem).wait()
                return c2
            lax.fori_loop(0, npad_ref[e], one_wait, 0)
            return carry
        lax.fori_loop(0, ne, per_expert, 0)

        def unused(blk, carry):
            cp = pltpu.make_async_copy(zero_ref, xs_ref.at[pl.ds(blk * MOE_ROWS, MOE_ROWS)], zsem)
            cp.start()
            cp.wait()
            return carry
        lax.fori_loop(nused_ref[0], nblk, unused, 0)

    def issue(t, carry):
        for kk in range(TOP_K):
            pltpu.make_async_copy(x_ref.at[pl.ds(t, 1)], xs_ref.at[pl.ds(dest_ref[kk, t], 1)], sem).start(priority=kk % 2)
        return carry
    lax.fori_loop(0, tm, issue, 0)

    def drain(t, carry):
        for kk in range(TOP_K):
            pltpu.make_async_copy(x_ref.at[pl.ds(0, 1)], xs_ref.at[pl.ds(0, 1)], sem).wait()
        return carry
    lax.fori_loop(0, tm, drain, 0)


def moe_dispatch(xf, dest, npad, pad_fill_start, nused, cap, tm=256):
    n, d = xf.shape
    tm = min(tm, n)
    smem = pl.BlockSpec(memory_space=pltpu.SMEM)
    return pl.pallas_call(
        _dispatch_kernel,
        out_shape=jax.ShapeDtypeStruct((cap, d), F32),
        grid=(n // tm,),
        in_specs=[pl.BlockSpec((TOP_K, tm), lambda i: (0, i), memory_space=pltpu.SMEM), smem, smem, smem,
                  pl.BlockSpec((tm, d), lambda i: (i, 0)), _const_spec((MOE_ROWS, d))],
        out_specs=pl.BlockSpec(memory_space=pl.ANY),
        scratch_shapes=[pltpu.SemaphoreType.DMA, pltpu.SemaphoreType.DMA],
        compiler_params=_cparams(("arbitrary",)),
        name="moe_dispatch",
    )(dest, npad, pad_fill_start, nused, xf, jnp.zeros((MOE_ROWS, d), F32))


PAIR_TILE = 2 * LANES


def _expert_changed(be_ref, i):
    return (i == 0) | (be_ref[i] != be_ref[jnp.maximum(i - 1, 0)])


def _next_weights(be_ref, nxt_ref, i, layer, w_hbm, wbuf_ref, sem, use):
    copy = lambda e: pltpu.make_async_copy(w_hbm.at[layer, e], wbuf_ref, sem)
    e = be_ref[i]

    @pl.when(i == 0)
    def _():
        copy(e).start()

    copy(e).wait()
    use()
    nx = nxt_ref[e]

    @pl.when(nx >= 0)
    def _():
        copy(nx).start()


def _expert_kernel(layer, be_ref, nused_ref, nxt_ref, xs_ref, wgu_hbm, bgu_ref, wdn_hbm, bdn_ref, y_ref,
                   wgu_buf, wperm_ref, bperm_ref, wdn_buf, wd_ref, act_ref, sem_gu, sem_dn):
    i = pl.program_id(0)
    active = i < nused_ref[0]
    ntile = wgu_buf.shape[1] // PAIR_TILE

    def deinterleave():
        r = lax.broadcasted_iota(jnp.int32, (PAIR_TILE, PAIR_TILE), 0)
        c = lax.broadcasted_iota(jnp.int32, (PAIR_TILE, PAIR_TILE), 1)
        perm = (r == jnp.where(c < LANES, 2 * c, 2 * (c - LANES) + 1)).astype(BF16)
        for t in range(ntile):
            sl = slice(t * PAIR_TILE, (t + 1) * PAIR_TILE)
            w = wgu_buf[:, sl].astype(BF16)
            wperm_ref[:, sl] = jnp.dot(w, perm, preferred_element_type=F32).astype(BF16)
            bias = jnp.broadcast_to(bgu_ref[0, 0, :, sl], (8, PAIR_TILE))
            bperm_ref[:, sl] = _dotx(bias, perm, 3, 1)

    def cast():
        wd_ref[...] = wdn_buf[...].astype(BF16)

    @pl.when(active & _expert_changed(be_ref, i))
    def _():
        _next_weights(be_ref, nxt_ref, i, layer, wgu_hbm, wgu_buf, sem_gu, deinterleave)
        _next_weights(be_ref, nxt_ref, i, layer, wdn_hbm, wdn_buf, sem_dn, cast)

    @pl.when(active)
    def _():
        x = xs_ref[...].astype(BF16)
        for t in range(ntile):
            sl = slice(t * PAIR_TILE, (t + 1) * PAIR_TILE)
            h = jnp.dot(x, wperm_ref[:, sl], preferred_element_type=F32) + bperm_ref[0:1, sl]
            gate = jnp.minimum(h[:, :LANES], SWIGLU_LIMIT)
            up = jnp.clip(h[:, LANES:], -SWIGLU_LIMIT, SWIGLU_LIMIT)
            act = (up + 1.0) * gate * _sigmoid(SWIGLU_ALPHA * gate)
            act_ref[:, t * LANES:(t + 1) * LANES] = act.astype(BF16)
        y_ref[...] = jnp.dot(act_ref[...], wd_ref[...], preferred_element_type=F32) + bdn_ref[0, 0]

    @pl.when(jnp.logical_not(active))
    def _():
        y_ref[...] = jnp.zeros_like(y_ref)


def moe_experts(xs, block_expert, nused, next_expert, layer, w_gu, b_gu, w_dn, b_dn):
    cap, d = xs.shape
    nl, ne, _, d2 = w_gu.shape
    dff = d2 // 2
    nblk = cap // MOE_ROWS
    blk = lambda i, be, nu, nx: (i, 0)
    bias = lambda n: pl.BlockSpec((1, 1, 1, n), lambda i, be, nu, nx: (layer, be[i], 0, 0))
    hbm = pl.BlockSpec(memory_space=pl.ANY)
    return pl.pallas_call(
        functools.partial(_expert_kernel, layer),
        out_shape=jax.ShapeDtypeStruct((cap, d), F32),
        grid_spec=pltpu.PrefetchScalarGridSpec(
            num_scalar_prefetch=3, grid=(nblk,),
            in_specs=[pl.BlockSpec((MOE_ROWS, d), blk), hbm, bias(d2), hbm, bias(d)],
            out_specs=pl.BlockSpec((MOE_ROWS, d), blk),
            scratch_shapes=[pltpu.VMEM((d, d2), F32), pltpu.VMEM((d, d2), BF16), pltpu.VMEM((8, d2), F32),
                            pltpu.VMEM((dff, d), F32), pltpu.VMEM((dff, d), BF16), pltpu.VMEM((MOE_ROWS, dff), BF16),
                            pltpu.SemaphoreType.DMA, pltpu.SemaphoreType.DMA]),
        compiler_params=_cparams(("arbitrary",)),
        name="moe_experts",
    )(block_expert, nused, next_expert, xs, w_gu, b_gu.reshape(nl, ne, 1, d2), w_dn, b_dn.reshape(nl, ne, 1, d))


def _plan_kernel(idx_ref, rank_ref, cnt_ref, dest_ref):
    ne = cnt_ref.shape[0]
    tm = idx_ref.shape[1]
    padded = ((cnt_ref[...] + (MOE_ROWS - 1)) >> MOE_ROWS_LOG2) << MOE_ROWS_LOG2
    r = lax.broadcasted_iota(jnp.int32, (ne, ne), 0)
    c = lax.broadcasted_iota(jnp.int32, (ne, ne), 1)
    pad_start = _dotx((c < r).astype(BF16), padded.astype(F32), 1, 3)[:, 0:1]
    eidx = lax.broadcasted_iota(jnp.int32, (ne, tm), 0)
    for kk in range(TOP_K):
        start = jnp.sum(jnp.where(eidx == idx_ref[kk:kk + 1, :], pad_start, 0.0), axis=0, keepdims=True)
        dest_ref[kk:kk + 1, :] = rank_ref[kk:kk + 1, :] + start.astype(jnp.int32)


def moe_plan(idx, rank, cnt, tm=2048):
    k, n = idx.shape
    ne = cnt.shape[0]
    tm = min(tm, n)
    spec = pl.BlockSpec((k, tm), lambda i: (0, i))
    return pl.pallas_call(
        _plan_kernel,
        out_shape=jax.ShapeDtypeStruct((k, n), jnp.int32),
        grid=(n // tm,),
        in_specs=[spec, spec, _const_spec((ne, LANES))],
        out_specs=spec,
        compiler_params=_cparams(("parallel",)),
        name="moe_plan",
    )(idx, rank, cnt)


def _combine_kernel(dest_ref, y_ref, gate_ref, x_ref, g_ref, b_ref, o_ref, buf_ref, sem):
    tm = x_ref.shape[0]

    def issue(t, carry):
        for kk in range(TOP_K):
            pltpu.make_async_copy(y_ref.at[pl.ds(dest_ref[kk, t], 1)], buf_ref.at[kk, pl.ds(t, 1)], sem).start(
                priority=kk % 2)
        return carry
    lax.fori_loop(0, tm, issue, 0)

    def drain(t, carry):
        for kk in range(TOP_K):
            pltpu.make_async_copy(y_ref.at[pl.ds(0, 1)], buf_ref.at[kk, pl.ds(0, 1)], sem).wait()
        return carry
    lax.fori_loop(0, tm, drain, 0)

    f = None
    for kk in range(TOP_K):
        t = buf_ref[kk] * gate_ref[:, kk:kk + 1]
        f = t if f is None else f + t
    o_ref[...] = _layer_norm(DEEPNORM_ALPHA * x_ref[...] + f, g_ref[...], b_ref[...])


def moe_combine_ln(y, dest, gate_t, xf, g, b, tm=256):
    n, d = xf.shape
    tm = min(tm, n)
    return pl.pallas_call(
        _combine_kernel,
        out_shape=jax.ShapeDtypeStruct((n, d), F32),
        grid=(n // tm,),
        in_specs=[pl.BlockSpec((TOP_K, tm), lambda i: (0, i), memory_space=pltpu.SMEM),
                  pl.BlockSpec(memory_space=pl.ANY),
                  pl.BlockSpec((tm, TOP_K), lambda i: (i, 0)),
                  pl.BlockSpec((tm, d), lambda i: (i, 0)), _const_spec((1, d)), _const_spec((1, d))],
        out_specs=pl.BlockSpec((tm, d), lambda i: (i, 0)),
        scratch_shapes=[pltpu.VMEM((TOP_K, tm, d), F32), pltpu.SemaphoreType.DMA],
        compiler_params=_cparams(("arbitrary",)),
        name="moe_combine_layernorm",
    )(dest, y, gate_t, xf, g.reshape(1, d), b.reshape(1, d))


def moe_ffn_ln(xf, layer, w_r, b_r, w_gu, b_gu, w_dn, b_dn, g, b):
    n, d = xf.shape
    ne = w_r.shape[1]
    idx, gate, rank, cnt = moe_route(xf, w_r, b_r)
    dest = moe_plan(idx, rank, cnt)
    counts = cnt[:, 0]
    padded = -(-counts // MOE_ROWS) * MOE_ROWS
    pad_end = jnp.cumsum(padded)
    pad_start = pad_end - padded
    cap = n * TOP_K + ne * MOE_ROWS
    nblk = cap // MOE_ROWS
    nused = (pad_end[-1] // MOE_ROWS).astype(jnp.int32).reshape(1)
    blk_row = jnp.arange(nblk, dtype=jnp.int32) * MOE_ROWS
    block_expert = jnp.minimum(jnp.sum(blk_row[:, None] >= pad_end[None, :], axis=1), ne - 1).astype(jnp.int32)
    xs = moe_dispatch(xf, dest, (padded - counts).astype(jnp.int32), (pad_start + counts).astype(jnp.int32),
                      nused, cap)
    eidx = jnp.arange(ne, dtype=jnp.int32)
    later = (eidx[None, :] > eidx[:, None]) & (counts[None, :] > 0)
    next_expert = jnp.min(jnp.where(later, eidx[None, :], ne), axis=1)
    next_expert = jnp.where(next_expert < ne, next_expert, -1).astype(jnp.int32)
    y = moe_experts(xs, block_expert, nused, next_expert, layer, w_gu, b_gu, w_dn, b_dn)
    return moe_combine_ln(y, dest, gate.T, xf, g, b)


def kernel(x, mem, ev_w_in, ev_mu, rk_w0, rk_w2, rk_a0, rk_a2, rk_g2, rk_kk, rk_ka, rk_rk, rk_gn_g, rk_gn_b,
           gd_conv, gd_a_log, gd_dt_bias, gd_norm_g, ev_w_out, od_w_in, cv_w, cv_b, cv_ln_g, cv_ln_b, pl_w,
           pl_scale, od_w_out, xa_wq, xa_wk, xa_wv, xa_wo, moe_wr, moe_br, moe_wgu, moe_bgu, moe_wdn, moe_bdn,
           ln_g, ln_b):
    nb, s, d = x.shape
    n = nb * s
    m = mem.shape[1]
    xf = x.reshape(n, d)
    memf = mem.reshape(nb * m, d)
    for layer in range(DEPTH):
        i = layer // 2
        if layer % 2 == 0:
            rwkv_cols = ev_mu.shape[1]
            p3 = matmul(xf, ev_w_in, i, tm=IN_PROJ_ROWS).reshape(nb, s, -1)
            ya = rwkv_mix(p3, ev_mu[i], rk_w0[i], rk_w2[i], rk_a0[i], rk_a2[i], rk_g2[i], rk_kk[i], rk_ka[i],
                          rk_rk[i].reshape(-1), rk_gn_g[i], rk_gn_b[i])
            yb = gdn_mix(p3, rwkv_cols, gd_conv[i], gd_a_log[i], gd_dt_bias[i], gd_norm_g[i])
            mixed = [ya.reshape(n, -1), yb.reshape(n, -1)]
            w_out = ev_w_out[i]
        else:
            p3 = matmul(xf, od_w_in, i, tm=IN_PROJ_ROWS).reshape(nb, s, -1)
            mixed = [odd_mix(p3, cv_w[i], cv_b[i], cv_ln_g[i], cv_ln_b[i], pl_w[i], pl_scale[i]).reshape(n, -1)]
            w_out = od_w_out[i]
        xf = mm_res_ln(mixed, w_out.astype(BF16), xf, ln_g[layer, 0], ln_b[layer, 0])
        k3 = matmul(memf, xa_wk, layer).reshape(nb, m, d)
        v3 = matmul(memf, xa_wv, layer).reshape(nb, m, d)
        xf = cross_attention_ln(xf.reshape(nb, s, d), k3, v3, xa_wq[layer].astype(BF16), xa_wo[layer].astype(BF16),
                                ln_g[layer, 1], ln_b[layer, 1]).reshape(n, d)
        xf = moe_ffn_ln(xf, layer, moe_wr[layer], moe_br[layer], moe_wgu, moe_bgu, moe_wdn, moe_bdn,
                        ln_g[layer, 2], ln_b[layer, 2])
    return xf.reshape(nb, s, d)
```
